```python
import math, functools
import jax, jax.numpy as jnp
from jax import lax
import numpy as np

D_MODEL = 2048
BATCH = 2
SEQ = 4096
DEPTH = 1
DEC_BATCH = 128
DEC_SEQ = 1
PAST_LEN = 2048
PAGE_SIZE = 128

A_HEADS = 8
A_HD = 64
A_KD = 2 * A_HD
A_VD = 2 * A_HD
A_WIDTH = A_HEADS * A_VD
B_HEADS = 8
B_DK = 128
B_DV = 128
B_FWIDTH = B_HEADS * B_DK
B_WIDTH = B_HEADS * B_DV
D_FF = 5632
ROPE_THETA = 10000.0
EPS = 1e-6
Q_BLOCK = 128
CHUNK = 64
POOL_NUM = 5
POOL_DEN = 4
IN_SPLITS = (A_HEADS * A_KD, A_HEADS * A_KD, A_WIDTH, B_FWIDTH, B_FWIDTH, B_WIDTH, B_WIDTH, D_MODEL, D_MODEL)
IN_WIDTH = sum(IN_SPLITS)

kernel_name = 'hybrid_diffattn_hgrn2_macaron_step'


def rmsnorm(x, g):
    x32 = x.astype(jnp.float32)
    y = x32 * lax.rsqrt(jnp.mean(x32 * x32, axis=-1, keepdims=True) + EPS)
    return (y * g.astype(jnp.float32)).astype(x.dtype)


def rope(x, pos):
    half = A_HD // 2
    inv = ROPE_THETA ** (-jnp.arange(half, dtype=jnp.float32) / half)
    ang = pos.astype(jnp.float32)[:, None] * inv[None, :]
    cos = jnp.cos(ang)[:, None, None, :]
    sin = jnp.sin(ang)[:, None, None, :]
    x32 = x.astype(jnp.float32)
    x1, x2 = x32[..., :half], x32[..., half:]
    return jnp.concatenate([x1 * cos - x2 * sin, x2 * cos + x1 * sin], axis=-1).astype(x.dtype)


def swiglu(x, w_gate, w_up, w_down):
    return (jax.nn.silu(x @ w_gate) * (x @ w_up)) @ w_down


def diff_combine(s, lam):
    p = jax.nn.softmax(s, axis=-1)
    return p[:, :, 0] - lam * p[:, :, 1]


def prompt_attend(q, k, v, lam):
    b, t = q.shape[:2]
    nb = t // Q_BLOCK
    qb = q.reshape(b, nb, Q_BLOCK, A_HEADS, 2, A_HD).swapaxes(0, 1)
    kpos = jnp.arange(t)

    def block(args):
        qi, i = args
        s = jnp.einsum('bqhcd,bkhcd->bhcqk', qi, k, preferred_element_type=jnp.float32)
        qpos = i * Q_BLOCK + jnp.arange(Q_BLOCK)
        s = jnp.where(qpos[:, None] >= kpos[None, :], s, -jnp.inf)
        pd = diff_combine(s, lam)
        return jnp.einsum('bhqk,bkhd->bqhd', pd.astype(v.dtype), v)

    o = lax.map(block, (qb, jnp.arange(nb)))
    return o.swapaxes(0, 1).reshape(b, t, A_HEADS, A_VD)


def sample_attend(q, k, v, lam, k_past, v_past):
    t = q.shape[1]
    p_len = k_past.shape[1]
    s_past = jnp.einsum('bqhcd,bkhcd->bhcqk', q, k_past, preferred_element_type=jnp.float32)
    s_new = jnp.einsum('bqhcd,bkhcd->bhcqk', q, k, preferred_element_type=jnp.float32)
    causal = jnp.tril(jnp.ones((t, t), dtype=bool))
    s_new = jnp.where(causal, s_new, -jnp.inf)
    pd = diff_combine(jnp.concatenate([s_past, s_new], axis=-1), lam)
    return (jnp.einsum('bhqk,bkhd->bqhd', pd[..., :p_len].astype(v.dtype), v_past)
            + jnp.einsum('bhqk,bkhd->bqhd', pd[..., p_len:].astype(v.dtype), v))


def hgrn_chunk(state, xs):
    q, k, v, logf = xs
    c = q.shape[2]
    bcum = jnp.cumsum(logf, axis=2)
    o_inter = jnp.einsum('bhtk,bhkv->bhtv', q * jnp.exp(bcum), state)
    tri = jnp.tril(jnp.ones((c, c), dtype=bool))[:, :, None]
    rel = bcum[:, :, :, None, :] - bcum[:, :, None, :, :]
    decay = jnp.where(tri, jnp.exp(jnp.where(tri, rel, 0.0)), 0.0)
    attn = jnp.einsum('bhtk,bhsk,bhtsk->bhts', q, k, decay)
    o_intra = jnp.einsum('bhts,bhsv->bhtv', attn, v)
    b_last = bcum[:, :, -1]
    new_state = (jnp.exp(b_last)[..., None] * state
                 + jnp.einsum('bhsk,bhsv->bhkv', k * jnp.exp(b_last[:, :, None, :] - bcum), v))
    return new_state, o_inter + o_intra


def hgrn_recurrence(s0, q, k, v, logf):
    b, t = q.shape[:2]
    c = math.gcd(t, CHUNK)
    n = t // c

    def to_chunks(a):
        return a.reshape(b, n, c, B_HEADS, a.shape[-1]).transpose(1, 0, 3, 2, 4)

    s_fin, o = lax.scan(hgrn_chunk, s0, (to_chunks(q), to_chunks(k), to_chunks(v), to_chunks(logf)))
    return s_fin, o.transpose(1, 0, 3, 2, 4).reshape(b, t, B_HEADS, B_DV)


def decoder_layer(x, pos, attend, s0, lp, lb, lam_init):
    b, t, _ = x.shape
    f32 = jnp.float32
    x = x + 0.5 * swiglu(rmsnorm(x, lp['ffn1_norm']), lp['ffn1_w_gate'], lp['ffn1_w_up'], lp['ffn1_w_down'])
    h = rmsnorm(x, lp['mix_norm'])
    z = h @ lp['w_in']
    split_points = [int(s) for s in np.cumsum(IN_SPLITS)[:-1]]
    qa, ka, va, qb, fb, ib, og, ga, gb = jnp.split(z, split_points, axis=-1)
    qa = rope(rmsnorm(qa.reshape(b, t, A_HEADS, 2, A_HD), lp['q_norm']), pos)
    ka = rope(rmsnorm(ka.reshape(b, t, A_HEADS, 2, A_HD), lp['k_norm']), pos)
    va = va.reshape(b, t, A_HEADS, A_VD)
    lam = (jnp.exp(jnp.sum(lp['lambda_q1'].astype(f32) * lp['lambda_k1'].astype(f32)))
           - jnp.exp(jnp.sum(lp['lambda_q2'].astype(f32) * lp['lambda_k2'].astype(f32))) + lam_init)
    oa = attend(qa * (A_HD ** -0.5), ka, va, lam)
    oa = (rmsnorm(oa, lp['attn_sub_norm']) * (1.0 - lam_init)).reshape(b, t, A_WIDTH)
    lbh = lb.reshape(B_HEADS, B_DK)
    f = lbh + (1.0 - lbh) * jax.nn.sigmoid(fb.astype(f32).reshape(b, t, B_HEADS, B_DK))
    s_new, ob = hgrn_recurrence(s0, qb.astype(f32).reshape(b, t, B_HEADS, B_DK), 1.0 - f,
                                ib.astype(f32).reshape(b, t, B_HEADS, B_DV), jnp.log(f))
    ob = rmsnorm(ob.astype(x.dtype), lp['hgrn_out_norm']).reshape(b, t, B_WIDTH) * jax.nn.silu(og)
    merged = jax.nn.sigmoid(ga) * (oa @ lp['w_proj_a']) + jax.nn.sigmoid(gb) * (ob @ lp['w_proj_b'])
    x = x + merged @ lp['w_out']
    x = x + 0.5 * swiglu(rmsnorm(x, lp['ffn2_norm']), lp['ffn2_w_gate'], lp['ffn2_w_up'], lp['ffn2_w_down'])
    return x, ka.reshape(b, t, A_HEADS, A_KD), va, s_new.astype(x.dtype)


def setup_inputs(seed: int = 0) -> dict:
    key = jax.random.key(seed)
    f32 = jnp.float32
    counter = [0]

    def nk():
        counter[0] += 1
        return jax.random.fold_in(key, counter[0])

    def normal(shape, scale=1.0):
        return jax.random.normal(nk(), shape, f32) * scale

    def gain(shape):
        return 1.0 + normal(shape, 0.02)

    n_pages = PAST_LEN // PAGE_SIZE
    n_used = DEC_BATCH * n_pages
    n_pool = (n_used * POOL_NUM) // POOL_DEN
    page_table = jax.random.permutation(nk(), n_pool)[:n_used].reshape(DEC_BATCH, n_pages).astype(jnp.int32)
    return {
        'x_prompt': normal((BATCH, SEQ, D_MODEL)),
        'x_sample': normal((DEC_BATCH, DEC_SEQ, D_MODEL)),
        'cache_k': normal((DEPTH, n_pool, PAGE_SIZE, A_HEADS, A_KD)),
        'cache_v': normal((DEPTH, n_pool, PAGE_SIZE, A_HEADS, A_VD)),
        'state_hgrn': normal((DEPTH, DEC_BATCH, B_HEADS, B_DK, B_DV), 0.5),
        'page_table': page_table,
        'ffn1_norm': gain((DEPTH, D_MODEL)),
        'ffn1_w_gate': normal((DEPTH, D_MODEL, D_FF), D_MODEL ** -0.5),
        'ffn1_w_up': normal((DEPTH, D_MODEL, D_FF), D_MODEL ** -0.5),
        'ffn1_w_down': normal((DEPTH, D_FF, D_MODEL), D_FF ** -0.5),
        'mix_norm': gain((DEPTH, D_MODEL)),
        'w_in': normal((DEPTH, D_MODEL, IN_WIDTH), D_MODEL ** -0.5),
        'q_norm': gain((DEPTH, A_HD)),
        'k_norm': gain((DEPTH, A_HD)),
        'lambda_q1': normal((DEPTH, A_HD), 0.1),
        'lambda_k1': normal((DEPTH, A_HD), 0.1),
        'lambda_q2': normal((DEPTH, A_HD), 0.1),
        'lambda_k2': normal((DEPTH, A_HD), 0.1),
        'attn_sub_norm': gain((DEPTH, A_VD)),
        'hgrn_lower_bounds': normal((DEPTH + 1, B_FWIDTH), 0.1),
        'hgrn_out_norm': gain((DEPTH, B_DV)),
        'w_proj_a': normal((DEPTH, A_WIDTH, D_MODEL), A_WIDTH ** -0.5),
        'w_proj_b': normal((DEPTH, B_WIDTH, D_MODEL), B_WIDTH ** -0.5),
        'w_out': normal((DEPTH, D_MODEL, D_MODEL), D_MODEL ** -0.5),
        'ffn2_norm': gain((DEPTH, D_MODEL)),
        'ffn2_w_gate': normal((DEPTH, D_MODEL, D_FF), D_MODEL ** -0.5),
        'ffn2_w_up': normal((DEPTH, D_MODEL, D_FF), D_MODEL ** -0.5),
        'ffn2_w_down': normal((DEPTH, D_FF, D_MODEL), D_FF ** -0.5),
    }


def reference(x_prompt, x_sample, cache_k, cache_v, state_hgrn, page_table,
              ffn1_norm, ffn1_w_gate, ffn1_w_up, ffn1_w_down,
              mix_norm, w_in, q_norm, k_norm, lambda_q1, lambda_k1, lambda_q2, lambda_k2,
              attn_sub_norm, hgrn_lower_bounds, hgrn_out_norm, w_proj_a, w_proj_b, w_out,
              ffn2_norm, ffn2_w_gate, ffn2_w_up, ffn2_w_down):
    f32 = jnp.float32
    batch, t_prompt = x_prompt.shape[:2]
    dec_batch, t_sample = x_sample.shape[:2]
    past_len = page_table.shape[1] * PAGE_SIZE
    pos_prompt = jnp.arange(t_prompt)
    pos_sample = past_len + jnp.arange(t_sample)
    lower_bounds = jnp.cumsum(jax.nn.softmax(hgrn_lower_bounds.astype(f32), axis=0), axis=0)
    y_p, y_s = x_prompt, x_sample
    k_p, v_p, k_s, v_s, st_p, st_s = [], [], [], [], [], []
    for l in range(DEPTH):
        lp = {
            'ffn1_norm': ffn1_norm[l], 'ffn1_w_gate': ffn1_w_gate[l], 'ffn1_w_up': ffn1_w_up[l],
            'ffn1_w_down': ffn1_w_down[l], 'mix_norm': mix_norm[l], 'w_in': w_in[l],
            'q_norm': q_norm[l], 'k_norm': k_norm[l], 'lambda_q1': lambda_q1[l], 'lambda_k1': lambda_k1[l],
            'lambda_q2': lambda_q2[l], 'lambda_k2': lambda_k2[l], 'attn_sub_norm': attn_sub_norm[l],
            'hgrn_out_norm': hgrn_out_norm[l], 'w_proj_a': w_proj_a[l], 'w_proj_b': w_proj_b[l],
            'w_out': w_out[l], 'ffn2_norm': ffn2_norm[l], 'ffn2_w_gate': ffn2_w_gate[l],
            'ffn2_w_up': ffn2_w_up[l], 'ffn2_w_down': ffn2_w_down[l],
        }
        lam_init = 0.8 - 0.6 * math.exp(-0.3 * l)
        s0 = jnp.zeros((batch, B_HEADS, B_DK, B_DV), f32)
        y_p, kr, vr, sr = decoder_layer(y_p, pos_prompt, prompt_attend, s0, lp, lower_bounds[l], lam_init)
        k_p.append(kr)
        v_p.append(vr)
        st_p.append(sr)
        k_past = cache_k[l][page_table].reshape(dec_batch, past_len, A_HEADS, 2, A_HD)
        v_past = cache_v[l][page_table].reshape(dec_batch, past_len, A_HEADS, A_VD)
        attend_s = functools.partial(sample_attend, k_past=k_past, v_past=v_past)
        y_s, kr, vr, sr = decoder_layer(y_s, pos_sample, attend_s, state_hgrn[l].astype(f32), lp,
                                        lower_bounds[l], lam_init)
        k_s.append(kr)
        v_s.append(vr)
        st_s.append(sr)
    return (y_p, y_s, jnp.stack(k_p), jnp.stack(v_p), jnp.stack(k_s), jnp.stack(v_s), jnp.stack(st_p), jnp.stack(st_s))
```

```python
import functools
import math

import jax
import jax.numpy as jnp
from jax import lax
from jax.experimental import pallas as pl
from jax.experimental.pallas import tpu as pltpu

F32 = jnp.float32
BF16 = jnp.bfloat16

EPS = 1e-6
ROPE_THETA = 10000.0
HEADS = 8
HEAD_W = 128
HALF = 64
PAGE = 128
LAM_INIT = 0.8 - 0.6 * math.exp(-0.3 * 0)
LANES = 128
VMEM_LIMIT = 56 * 1024 * 1024
NEG = -1e30


def _cparams(sem):
    return pltpu.CompilerParams(dimension_semantics=sem, vmem_limit_bytes=VMEM_LIMIT)


def _nt_dot(a, b):
    return lax.dot_general(a, b, (((1,), (1,)), ((), ())), preferred_element_type=F32)


def _rms_rows(x, g):
    ms = jnp.mean(x * x, axis=-1, keepdims=True)
    return x * lax.rsqrt(ms + EPS) * g


def _ffn_kernel(x_ref, g_ref, wg_ref, wu_ref, wd_ref, o_ref, h_ref):
    j = pl.program_id(1)

    @pl.when(j == 0)
    def _():
        x = x_ref[...]
        h_ref[...] = _rms_rows(x, g_ref[...]).astype(BF16)
        o_ref[...] = x

    h = h_ref[...]
    a = jnp.dot(h, wg_ref[...], preferred_element_type=F32)
    u = jnp.dot(h, wu_ref[...], preferred_element_type=F32)
    t = (a * jax.nn.sigmoid(a) * (0.5 * u)).astype(BF16)
    o_ref[...] += jnp.dot(t, wd_ref[...], preferred_element_type=F32)


def _ffn(x, g, wg, wu, wd, tm, tf):
    m, d = x.shape
    f = wg.shape[1]
    return pl.pallas_call(
        _ffn_kernel,
        grid=(m // tm, f // tf),
        in_specs=[
            pl.BlockSpec((tm, d), lambda i, j: (i, 0)),
            pl.BlockSpec((1, d), lambda i, j: (0, 0)),
            pl.BlockSpec((d, tf), lambda i, j: (0, j)),
            pl.BlockSpec((d, tf), lambda i, j: (0, j)),
            pl.BlockSpec((tf, d), lambda i, j: (j, 0)),
        ],
        out_specs=pl.BlockSpec((tm, d), lambda i, j: (i, 0)),
        out_shape=jax.ShapeDtypeStruct((m, d), F32),
        scratch_shapes=[pltpu.VMEM((tm, d), BF16)],
        compiler_params=_cparams(("parallel", "arbitrary")),
        name="ffn",
    )(x, g, wg, wu, wd)


def _normmm_kernel(x_ref, g_ref, w_ref, o_ref, h_ref):
    @pl.when(pl.program_id(1) == 0)
    def _():
        h_ref[...] = _rms_rows(x_ref[...], g_ref[...]).astype(BF16)

    o_ref[...] = jnp.dot(h_ref[...], w_ref[...], preferred_element_type=F32).astype(o_ref.dtype)


def _normmm(x, g, w, out_dtype, tm, tn):
    m, d = x.shape
    n = w.shape[1]
    return pl.pallas_call(
        _normmm_kernel,
        grid=(m // tm, n // tn),
        in_specs=[
            pl.BlockSpec((tm, d), lambda i, j: (i, 0)),
            pl.BlockSpec((1, d), lambda i, j: (0, 0)),
            pl.BlockSpec((d, tn), lambda i, j: (0, j)),
        ],
        out_specs=pl.BlockSpec((tm, tn), lambda i, j: (i, j)),
        out_shape=jax.ShapeDtypeStruct((m, n), out_dtype),
        scratch_shapes=[pltpu.VMEM((tm, d), BF16)],
        compiler_params=_cparams(("parallel", "arbitrary")),
        name="normmm",
    )(x, g, w)


def _group_ms(x):
    r = lax.broadcasted_iota(jnp.int32, (LANES, LANES), 0)
    c = lax.broadcasted_iota(jnp.int32, (LANES, LANES), 1)
    grp = jnp.where((r >> 6) == (c >> 6), 1.0, 0.0).astype(BF16)
    x2 = x * x
    hi = x2.astype(BF16)
    r1 = x2 - hi.astype(F32)
    mid = r1.astype(BF16)
    lo = (r1 - mid.astype(F32)).astype(BF16)
    s = (jnp.dot(hi, grp, preferred_element_type=F32)
         + jnp.dot(mid, grp, preferred_element_type=F32)
         + jnp.dot(lo, grp, preferred_element_type=F32))
    return s * (1.0 / HALF)


def _norm_rope(x, g, cos, sin):
    y = x * lax.rsqrt(_group_ms(x) + EPS) * g
    lane = lax.broadcasted_iota(jnp.int32, y.shape, 1)
    first = (lane & (HALF - 1)) < (HALF // 2)
    rot = jnp.where(first, -pltpu.roll(y, LANES - HALF // 2, 1), pltpu.roll(y, HALF // 2, 1))
    return y * cos + rot * sin


def _prep_prompt_kernel(q_ref, k_ref, v_ref, cos_ref, sin_ref, gq_ref, gk_ref,
                        qb_ref, kb_ref, vb_ref, kf_ref, vf_ref):
    cos = cos_ref[...]
    sin = sin_ref[...]
    q = _norm_rope(q_ref[...], gq_ref[...], cos, sin) * (HALF ** -0.5)
    k = _norm_rope(k_ref[...], gk_ref[...], cos, sin)
    v = v_ref[...]
    qb_ref[...] = q.astype(BF16)
    kb_ref[...] = k.astype(BF16)
    vb_ref[...] = v.astype(BF16)
    kf_ref[...] = k
    vf_ref[...] = v


def _prep_prompt(z, cos, sin, gq, gk, batch, seq, tm):
    nt = seq // tm
    row = lambda b, i, h: b * nt + i
    hm = pl.BlockSpec((None, None, tm, HEAD_W), lambda b, i, h: (b, h, i, 0))
    flat = pl.BlockSpec((tm, HEAD_W), lambda b, i, h: (row(b, i, h), h))
    vec = pl.BlockSpec((1, HEAD_W), lambda b, i, h: (0, 0))
    tab = pl.BlockSpec((tm, HEAD_W), lambda b, i, h: (i, 0))
    hm_shape = jax.ShapeDtypeStruct((batch, HEADS, seq, HEAD_W), BF16)
    flat_shape = jax.ShapeDtypeStruct((batch * seq, HEADS * HEAD_W), F32)
    return pl.pallas_call(
        _prep_prompt_kernel,
        grid=(batch, nt, HEADS),
        in_specs=[
            pl.BlockSpec((tm, HEAD_W), lambda b, i, h: (row(b, i, h), h)),
            pl.BlockSpec((tm, HEAD_W), lambda b, i, h: (row(b, i, h), HEADS + h)),
            pl.BlockSpec((tm, HEAD_W), lambda b, i, h: (row(b, i, h), 2 * HEADS + h)),
            tab, tab, vec, vec,
        ],
        out_specs=[hm, hm, hm, flat, flat],
        out_shape=[hm_shape, hm_shape, hm_shape, flat_shape, flat_shape],
        compiler_params=_cparams(("parallel", "parallel", "parallel")),
        name="prep_prompt",
    )(z, z, z, cos, sin, gq, gk)


def _prep_sample_kernel(q_ref, k_ref, v_ref, cos_ref, sin_ref, gq_ref, gk_ref,
                        qf_ref, kf_ref, vf_ref):
    cos = cos_ref[...]
    sin = sin_ref[...]
    qf_ref[...] = _norm_rope(q_ref[...], gq_ref[...], cos, sin) * (HALF ** -0.5)
    kf_ref[...] = _norm_rope(k_ref[...], gk_ref[...], cos, sin)
    vf_ref[...] = v_ref[...]


def _prep_sample(z, cos, sin, gq, gk):
    m = z.shape[0]
    blk = lambda off: pl.BlockSpec((m, HEAD_W), lambda h: (0, off + h))
    vec = pl.BlockSpec((1, HEAD_W), lambda h: (0, 0))
    tab = pl.BlockSpec((m, HEAD_W), lambda h: (0, 0))
    shape = jax.ShapeDtypeStruct((m, HEADS * HEAD_W), F32)
    return pl.pallas_call(
        _prep_sample_kernel,
        grid=(HEADS,),
        in_specs=[blk(0), blk(HEADS), blk(2 * HEADS), tab, tab, vec, vec],
        out_specs=[blk(0), blk(0), blk(0)],
        out_shape=[shape, shape, shape],
        compiler_params=_cparams(("parallel",)),
        name="prep_sample",
    )(z, z, z, cos, sin, gq, gk)


def _lambda(lam_ref):
    l = lam_ref[...]
    s1 = jnp.sum(l[0:1] * l[1:2], axis=1, keepdims=True)
    s2 = jnp.sum(l[2:3] * l[3:4], axis=1, keepdims=True)
    return jnp.exp(s1) - jnp.exp(s2) + LAM_INIT


def _pattn_kernel(lam_ref, q_ref, k_ref, v_ref, o_ref, m_ref, l_ref, acc_ref, *, tq):
    qi = pl.program_id(2)
    q = q_ref[...]
    lane = lax.broadcasted_iota(jnp.int32, q.shape, 1)
    zero = jnp.zeros_like(q)
    qs = jnp.concatenate([jnp.where(lane < HALF, q, zero),
                          jnp.where(lane >= HALF, q, zero)], axis=0)

    m_ref[...] = jnp.full(m_ref.shape, NEG, F32)
    l_ref[...] = jnp.zeros(l_ref.shape, F32)
    acc_ref[...] = jnp.zeros(acc_ref.shape, F32)

    def step(kb, masked):
        start = pl.multiple_of(kb * tq, tq)
        k = k_ref[pl.ds(start, tq), :]
        v = v_ref[pl.ds(start, tq), :]
        s = _nt_dot(qs, k)
        if masked:
            r = lax.broadcasted_iota(jnp.int32, s.shape, 0)
            c = lax.broadcasted_iota(jnp.int32, s.shape, 1)
            r = jnp.where(r >= tq, r - tq, r)
            s = jnp.where(r >= c, s, NEG)
        m_prev = m_ref[...]
        m_new = jnp.maximum(m_prev, jnp.max(s, axis=1, keepdims=True))
        alpha = jnp.exp(m_prev - m_new)
        p = jnp.exp(s - jnp.concatenate([m_new] * (tq // LANES), axis=1))
        lsum = p[:, 0:LANES]
        for cblk in range(1, tq // LANES):
            lsum = lsum + p[:, cblk * LANES:(cblk + 1) * LANES]
        l_ref[...] = alpha * l_ref[...] + lsum
        acc_ref[...] = alpha * acc_ref[...] + jnp.dot(p.astype(BF16), v, preferred_element_type=F32)
        m_ref[...] = m_new

    def body(kb, carry):
        step(kb, False)
        return carry

    lax.fori_loop(0, qi, body, 0)
    step(qi, True)

    l = jnp.sum(l_ref[...], axis=1, keepdims=True)
    o = acc_ref[...] / l
    o_ref[...] = o[:tq] - _lambda(lam_ref) * o[tq:]


def _pattn(lam, qb, kb, vb, tq):
    batch, heads, seq, w = qb.shape
    nq = seq // tq
    return pl.pallas_call(
        functools.partial(_pattn_kernel, tq=tq),
        grid=(batch, heads, nq),
        in_specs=[
            pl.BlockSpec(lam.shape, lambda b, h, i: (0, 0)),
            pl.BlockSpec((None, None, tq, w), lambda b, h, i: (b, h, i, 0)),
            pl.BlockSpec((None, None, seq, w), lambda b, h, i: (b, h, 0, 0)),
            pl.BlockSpec((None, None, seq, w), lambda b, h, i: (b, h, 0, 0)),
        ],
        out_specs=pl.BlockSpec((tq, w), lambda b, h, i: (b * nq + i, h)),
        out_shape=jax.ShapeDtypeStruct((batch * seq, heads * w), F32),
        scratch_shapes=[pltpu.VMEM((2 * tq, LANES), F32),
                        pltpu.VMEM((2 * tq, LANES), F32),
                        pltpu.VMEM((2 * tq, w), F32)],
        compiler_params=_cparams(("parallel", "parallel", "arbitrary")),
        name="prompt_attn",
    )(lam, qb, kb, vb)


def _sattn_kernel(pt_ref, lam_ref, q_ref, kn_ref, vn_ref, ck_hbm, cv_hbm, o_ref,
                  kbuf, vbuf, sem, *, n_pages):
    b = pl.program_id(0)
    nb = pl.num_programs(0)
    slot = b % 2

    def page_copies(seq, slot_):
        cps = []
        for p in range(n_pages):
            pg = pt_ref[seq, p]
            cps.append(pltpu.make_async_copy(
                ck_hbm.at[pg], kbuf.at[slot_, pl.ds(p * PAGE, PAGE), :], sem.at[0, slot_]))
            cps.append(pltpu.make_async_copy(
                cv_hbm.at[pg], vbuf.at[slot_, pl.ds(p * PAGE, PAGE), :], sem.at[1, slot_]))
        return cps

    @pl.when(b == 0)
    def _():
        for cp in page_copies(0, 0):
            cp.start()

    @pl.when(b + 1 < nb)
    def _():
        for cp in page_copies(b + 1, 1 - slot):
            cp.start()

    for cp in page_copies(b, slot):
        cp.wait()

    width = HEADS * HEAD_W
    q = q_ref[...]
    row = lax.broadcasted_iota(jnp.int32, (2 * HEADS, width), 0)
    grp = lax.broadcasted_iota(jnp.int32, (2 * HEADS, width), 1) >> 6
    qrows = jnp.where(row == (grp & 1) * HEADS + (grp >> 1), q, 0.0)

    kpast = kbuf[slot].astype(BF16)
    s = _nt_dot(qrows.astype(BF16), kpast)
    s_new = jnp.sum(qrows * kn_ref[...], axis=1, keepdims=True)
    m = jnp.maximum(jnp.max(s, axis=1, keepdims=True), s_new)
    p = jnp.exp(s - m)
    p_new = jnp.exp(s_new - m)
    inv = 1.0 / (jnp.sum(p, axis=1, keepdims=True) + p_new)
    lam = _lambda(lam_ref)
    w = p * inv
    w_new = p_new * inv
    pd = w[:HEADS] - lam * w[HEADS:]
    pd_new = w_new[:HEADS] - lam * w_new[HEADS:]
    vpast = vbuf[slot].astype(BF16)
    o = jnp.dot(pd.astype(BF16), vpast, preferred_element_type=F32) + pd_new * vn_ref[...]
    hrow = lax.broadcasted_iota(jnp.int32, o.shape, 0)
    hlane = lax.broadcasted_iota(jnp.int32, o.shape, 1) >> 7
    o_ref[...] = jnp.sum(jnp.where(hrow == hlane, o, 0.0), axis=0, keepdims=True)


def _sattn(page_table, lam, q, kn, vn, ck, cv):
    nb, n_pages = page_table.shape
    width = HEADS * HEAD_W
    past = n_pages * PAGE
    rowspec = pl.BlockSpec((None, 1, width), lambda b, pt: (b, 0, 0))
    grid_spec = pltpu.PrefetchScalarGridSpec(
        num_scalar_prefetch=1,
        grid=(nb,),
        in_specs=[
            pl.BlockSpec(lam.shape, lambda b, pt: (0, 0)),
            rowspec, rowspec, rowspec,
            pl.BlockSpec(memory_space=pl.ANY),
            pl.BlockSpec(memory_space=pl.ANY),
        ],
        out_specs=rowspec,
        scratch_shapes=[pltpu.VMEM((2, past, width), F32),
                        pltpu.VMEM((2, past, width), F32),
                        pltpu.SemaphoreType.DMA((2, 2))],
    )
    r3 = lambda a: a.reshape(nb, 1, width)
    out = pl.pallas_call(
        functools.partial(_sattn_kernel, n_pages=n_pages),
        grid_spec=grid_spec,
        out_shape=jax.ShapeDtypeStruct((nb, 1, width), F32),
        compiler_params=_cparams(("arbitrary",)),
        name="sample_attn",
    )(page_table, lam, r3(q), r3(kn), r3(vn), ck, cv)
    return out.reshape(nb, width)


def _lower_bound(lb_ref, sl):
    a = lb_ref[:, sl]
    mx = jnp.maximum(a[0:1], a[1:2])
    e0 = jnp.exp(a[0:1] - mx)
    e1 = jnp.exp(a[1:2] - mx)
    return e0 / (e0 + e1)


def _split3(x):
    hi = x.astype(BF16)
    r1 = x - hi.astype(F32)
    mid = r1.astype(BF16)
    lo = (r1 - mid.astype(F32)).astype(BF16)
    return hi, mid, lo


def _pair_ref(b, m, c):
    parts = [jnp.broadcast_to(b[2 * m * p + m - 1:2 * m * p + m, :], (2 * m, b.shape[1]))
             for p in range(c // (2 * m))]
    return parts[0] if len(parts) == 1 else jnp.concatenate(parts, axis=0)


def _hgrn_prompt_kernel(lb_ref, q_ref, f_ref, i_ref, o_ref, s_out_ref, st_ref, *, c):
    ci = pl.program_id(1)

    @pl.when(ci == 0)
    def _():
        st_ref[...] = jnp.zeros(st_ref.shape, F32)

    row = lax.broadcasted_iota(jnp.int32, (c, c), 0)
    col = lax.broadcasted_iota(jnp.int32, (c, c), 1)
    tri = jnp.where(col <= row, 1.0, 0.0).astype(BF16)
    same_sub = (row >> 4) == (col >> 4)
    causal = col <= row
    mask16 = (row >> 5) == (col >> 5)
    mask32 = (row >> 6) == (col >> 6)
    trow = lax.broadcasted_iota(jnp.int32, (c, HEAD_W), 0)
    sub = 16

    for h in range(HEADS):
        sl = slice(h * HEAD_W, (h + 1) * HEAD_W)
        lb = _lower_bound(lb_ref, sl)
        q = q_ref[:, sl]
        v = i_ref[:, sl]
        f = lb + (1.0 - lb) * jax.nn.sigmoid(f_ref[:, sl])
        k = 1.0 - f
        hi, mid, lo = _split3(jnp.log(f))
        b = (jnp.dot(tri, hi, preferred_element_type=F32)
             + jnp.dot(tri, mid, preferred_element_type=F32)
             + jnp.dot(tri, lo, preferred_element_type=F32))

        ref_d = jnp.concatenate(
            [jnp.zeros((sub, HEAD_W), F32)]
            + [jnp.broadcast_to(b[sub * j - 1:sub * j, :], (sub, HEAD_W)) for j in range(1, c // sub)],
            axis=0)
        arg_d = b - ref_d
        p_d = _nt_dot((q * jnp.exp(arg_d)).astype(BF16), (k * jnp.exp(-arg_d)).astype(BF16))
        att = jnp.where(same_sub, jnp.where(causal, p_d, 0.0), 0.0)
        for m, msk in ((16, mask16), (32, mask32), (64, None)):
            odd = ((trow >> int(math.log2(m))) & 1) == 1
            d = b - _pair_ref(b, m, c)
            e = jnp.exp(jnp.where(odd, d, -d))
            qh = jnp.where(odd, q * e, 0.0).astype(BF16)
            kh = jnp.where(odd, 0.0, k * e).astype(BF16)
            p_m = _nt_dot(qh, kh)
            att = att + (p_m if msk is None else jnp.where(msk, p_m, 0.0))

        b_last = b[c - 1:c, :]
        st = st_ref[h]
        o = (jnp.dot(att.astype(BF16), v.astype(BF16), preferred_element_type=F32)
             + _nt_dot((q * jnp.exp(b)).astype(BF16), st.astype(BF16)))
        o_ref[:, sl] = o
        kt = (k * jnp.exp(b_last - b)).astype(BF16)
        st_new = st * jnp.exp(b_last) + jnp.dot(v.T.astype(BF16), kt, preferred_element_type=F32)
        st_ref[h] = st_new

        @pl.when(ci == pl.num_programs(1) - 1)
        def _():
            s_out_ref[h] = st_new.T


def _hgrn_prompt(lb2, z, batch, seq, c):
    nc = seq // c
    width = HEADS * HEAD_W
    col = lambda off: pl.BlockSpec((c, width), lambda b, i: (b * nc + i, off))
    return pl.pallas_call(
        functools.partial(_hgrn_prompt_kernel, c=c),
        grid=(batch, nc),
        in_specs=[pl.BlockSpec(lb2.shape, lambda b, i: (0, 0)), col(3), col(4), col(5)],
        out_specs=[pl.BlockSpec((c, width), lambda b, i: (b * nc + i, 0)),
                   pl.BlockSpec((None, HEADS, HEAD_W, HEAD_W), lambda b, i: (b, 0, 0, 0))],
        out_shape=[jax.ShapeDtypeStruct((batch * seq, width), F32),
                   jax.ShapeDtypeStruct((batch, HEADS, HEAD_W, HEAD_W), F32)],
        scratch_shapes=[pltpu.VMEM((HEADS, HEAD_W, HEAD_W), F32)],
        compiler_params=_cparams(("parallel", "arbitrary")),
        name="hgrn_prompt",
    )(lb2, z, z, z)


def _hgrn_sample_kernel(lb_ref, q_ref, f_ref, i_ref, s_ref, o_ref, s_out_ref, *, rows):
    r_i = lax.broadcasted_iota(jnp.int32, (HEAD_W, HEAD_W), 0)
    c_i = lax.broadcasted_iota(jnp.int32, (HEAD_W, HEAD_W), 1)
    eye = r_i == c_i

    def to_col(x):
        return jnp.sum(jnp.where(eye, x, 0.0), axis=1, keepdims=True)

    for h in range(HEADS):
        sl = slice(h * HEAD_W, (h + 1) * HEAD_W)
        lb = _lower_bound(lb_ref, sl)
        f_all = lb + (1.0 - lb) * jax.nn.sigmoid(f_ref[:, sl])
        q_all = q_ref[:, sl]
        i_all = i_ref[:, sl]
        o_rows = []
        for r in range(rows):
            f_col = to_col(f_all[r:r + 1])
            q_col = to_col(q_all[r:r + 1])
            s_new = f_col * s_ref[r, h] + (1.0 - f_col) * i_all[r:r + 1]
            s_out_ref[r, h] = s_new
            o_rows.append(jnp.sum(q_col * s_new, axis=0, keepdims=True))
        o_ref[:, sl] = jnp.concatenate(o_rows, axis=0)


def _hgrn_sample(lb2, z, state, rows):
    nb = state.shape[0]
    width = HEADS * HEAD_W
    col = lambda off: pl.BlockSpec((rows, width), lambda i: (i, off))
    sspec = pl.BlockSpec((rows, HEADS, HEAD_W, HEAD_W), lambda i: (i, 0, 0, 0))
    return pl.pallas_call(
        functools.partial(_hgrn_sample_kernel, rows=rows),
        grid=(nb // rows,),
        in_specs=[pl.BlockSpec(lb2.shape, lambda i: (0, 0)), col(3), col(4), col(5), sspec],
        out_specs=[pl.BlockSpec((rows, width), lambda i: (i, 0)), sspec],
        out_shape=[jax.ShapeDtypeStruct((nb, width), F32),
                   jax.ShapeDtypeStruct(state.shape, F32)],
        compiler_params=_cparams(("parallel",)),
        name="hgrn_sample",
    )(lb2, z, z, z, state)


def _head_rms(x, g):
    parts = []
    for h in range(HEADS):
        blk = x[:, h * HEAD_W:(h + 1) * HEAD_W]
        parts.append(_rms_rows(blk, g))
    return jnp.concatenate(parts, axis=1)


def _merge_kernel(oa_ref, ob_ref, og_ref, ga_ref, gb_ref, na_ref, nb_ref, wa_ref, wb_ref,
                  o_ref, a_ref, b_ref):
    @pl.when(pl.program_id(1) == 0)
    def _():
        a_ref[...] = (_head_rms(oa_ref[...], na_ref[...]) * (1.0 - LAM_INIT)).astype(BF16)
        og = og_ref[...].astype(F32)
        b_ref[...] = (_head_rms(ob_ref[...], nb_ref[...]) * (og * jax.nn.sigmoid(og))).astype(BF16)

    pa = jnp.dot(a_ref[...], wa_ref[...], preferred_element_type=F32)
    pb = jnp.dot(b_ref[...], wb_ref[...], preferred_element_type=F32)
    o_ref[...] = (jax.nn.sigmoid(ga_ref[...].astype(F32)) * pa
                  + jax.nn.sigmoid(gb_ref[...].astype(F32)) * pb).astype(o_ref.dtype)


def _merge(oa, ob, zg, na, nb, wa, wb, tm, tn):
    m, w = oa.shape
    n = wa.shape[1]
    goff = w // tn
    return pl.pallas_call(
        _merge_kernel,
        grid=(m // tm, n // tn),
        in_specs=[
            pl.BlockSpec((tm, w), lambda i, j: (i, 0)),
            pl.BlockSpec((tm, w), lambda i, j: (i, 0)),
            pl.BlockSpec((tm, w), lambda i, j: (i, 0)),
            pl.BlockSpec((tm, tn), lambda i, j: (i, goff + j)),
            pl.BlockSpec((tm, tn), lambda i, j: (i, goff + n // tn + j)),
            pl.BlockSpec((1, HEAD_W), lambda i, j: (0, 0)),
            pl.BlockSpec((1, HEAD_W), lambda i, j: (0, 0)),
            pl.BlockSpec((w, tn), lambda i, j: (0, j)),
            pl.BlockSpec((w, tn), lambda i, j: (0, j)),
        ],
        out_specs=pl.BlockSpec((tm, tn), lambda i, j: (i, j)),
        out_shape=jax.ShapeDtypeStruct((m, n), BF16),
        scratch_shapes=[pltpu.VMEM((tm, w), BF16), pltpu.VMEM((tm, w), BF16)],
        compiler_params=_cparams(("parallel", "arbitrary")),
        name="merge",
    )(oa, ob, zg, zg, zg, na, nb, wa, wb)


def _resmm_kernel(x_ref, a_ref, w_ref, o_ref):
    o_ref[...] = x_ref[...] + jnp.dot(a_ref[...], w_ref[...], preferred_element_type=F32)


def _resmm(x, a, w, tm, tn):
    m, kdim = a.shape
    n = w.shape[1]
    return pl.pallas_call(
        _resmm_kernel,
        grid=(m // tm, n // tn),
        in_specs=[
            pl.BlockSpec((tm, tn), lambda i, j: (i, j)),
            pl.BlockSpec((tm, kdim), lambda i, j: (i, 0)),
            pl.BlockSpec((kdim, tn), lambda i, j: (0, j)),
        ],
        out_specs=pl.BlockSpec((tm, tn), lambda i, j: (i, j)),
        out_shape=jax.ShapeDtypeStruct((m, n), F32),
        compiler_params=_cparams(("parallel", "parallel")),
        name="resmm",
    )(x, a, w)


def _rope_tables(pos):
    half = HALF // 2
    inv = ROPE_THETA ** (-jnp.arange(half, dtype=F32) / half)
    ang = pos.astype(F32)[:, None] * inv[None, :]
    cos = jnp.tile(jnp.cos(ang), (1, LANES // half))
    sin = jnp.tile(jnp.sin(ang), (1, LANES // half))
    return cos, sin


def kernel(x_prompt, x_sample, cache_k, cache_v, state_hgrn, page_table, ffn1_norm, ffn1_w_gate, ffn1_w_up, ffn1_w_down, mix_norm, w_in, q_norm, k_norm, lambda_q1, lambda_k1, lambda_q2, lambda_k2, attn_sub_norm, hgrn_lower_bounds, hgrn_out_norm, w_proj_a, w_proj_b, w_out, ffn2_norm, ffn2_w_gate, ffn2_w_up, ffn2_w_down):
    batch, seq, d = x_prompt.shape
    nb = x_sample.shape[0]
    n_pages = page_table.shape[1]
    width = HEADS * HEAD_W
    n_a = 6 * width

    bf = lambda a: a.astype(BF16)
    w1g, w1u, w1d = bf(ffn1_w_gate[0]), bf(ffn1_w_up[0]), bf(ffn1_w_down[0])
    w2g, w2u, w2d = bf(ffn2_w_gate[0]), bf(ffn2_w_up[0]), bf(ffn2_w_down[0])
    w_in_a, w_in_g = bf(w_in[0][:, :n_a]), bf(w_in[0][:, n_a:])
    wa, wb, wo = bf(w_proj_a[0]), bf(w_proj_b[0]), bf(w_out[0])
    lam = jnp.concatenate([lambda_q1, lambda_k1, lambda_q2, lambda_k2], axis=0).astype(F32)
    gq = jnp.tile(q_norm.astype(F32), (1, 2))
    gk = jnp.tile(k_norm.astype(F32), (1, 2))
    lb2 = hgrn_lower_bounds.astype(F32)

    def front(x, tm, tn):
        x1 = _ffn(x, ffn1_norm, w1g, w1u, w1d, tm, 512)
        za = _normmm(x1, mix_norm, w_in_a, F32, tn, 512)
        zg = _normmm(x1, mix_norm, w_in_g, BF16, tn, 512)
        return x1, za, zg

    def back(x1, oa, ob, zg, tm):
        mg = _merge(oa, ob, zg, attn_sub_norm, hgrn_out_norm, wa, wb, tm, 512)
        x2 = _resmm(x1, mg, wo, tm, 512)
        return _ffn(x2, ffn2_norm, w2g, w2u, w2d, tm, 512)

    xp = x_prompt.reshape(batch * seq, d)
    x1p, zap, zgp = front(xp, 512, 1024)
    cos_p, sin_p = _rope_tables(jnp.arange(seq))
    qb, kb, vb, kf, vf = _prep_prompt(zap, cos_p, sin_p, gq, gk, batch, seq, 512)
    oa_p = _pattn(lam, qb, kb, vb, 512)
    ob_p, st_p = _hgrn_prompt(lb2, zap, batch, seq, 128)
    y_p = back(x1p, oa_p, ob_p, zgp, 512)

    xs = x_sample.reshape(nb, d)
    x1s, zas, zgs = front(xs, nb, nb)
    pos_s = jnp.full((nb,), n_pages * PAGE, jnp.int32)
    cos_s, sin_s = _rope_tables(pos_s)
    qs, ks, vs = _prep_sample(zas, cos_s, sin_s, gq, gk)
    ck = cache_k[0].reshape(cache_k.shape[1], PAGE, width)
    cv = cache_v[0].reshape(cache_v.shape[1], PAGE, width)
    oa_s = _sattn(page_table, lam, qs, ks, vs, ck, cv)
    ob_s, st_s = _hgrn_sample(lb2, zas, state_hgrn[0], 8)
    y_s = back(x1s, oa_s, ob_s, zgs, nb)

    return (y_p.reshape(batch, seq, d),
            y_s.reshape(nb, 1, d),
            kf.reshape(1, batch, seq, HEADS, HEAD_W),
            vf.reshape(1, batch, seq, HEADS, HEAD_W),
            ks.reshape(1, nb, 1, HEADS, HEAD_W),
            vs.reshape(1, nb, 1, HEADS, HEAD_W),
            st_p.reshape(1, batch, HEADS, HEAD_W, HEAD_W),
            st_s.reshape(1, nb, HEADS, HEAD_W, HEAD_W))
```

```python
import functools
import math

import jax
import jax.numpy as jnp
from jax import lax
from jax.experimental import pallas as pl
from jax.experimental.pallas import tpu as pltpu

F32 = jnp.float32
BF16 = jnp.bfloat16

EPS = 1e-6
ROPE_THETA = 10000.0
HEADS = 8
HEAD_W = 128
HALF = 64
PAGE = 128
LAM_INIT = 0.8 - 0.6 * math.exp(-0.3 * 0)
LANES = 128
VMEM_LIMIT = 56 * 1024 * 1024
NEG = -1e30


def _cparams(sem):
    return pltpu.CompilerParams(dimension_semantics=sem, vmem_limit_bytes=VMEM_LIMIT)


def _nt_dot(a, b):
    return lax.dot_general(a, b, (((1,), (1,)), ((), ())), preferred_element_type=F32)


def _rms_rows(x, g):
    ms = jnp.mean(x * x, axis=-1, keepdims=True)
    return x * lax.rsqrt(ms + EPS) * g


def _ffn_kernel(x_ref, g_ref, wg_ref, wu_ref, wd_ref, o_ref, h_ref):
    j = pl.program_id(1)

    @pl.when(j == 0)
    def _():
        x = x_ref[...]
        h_ref[...] = _rms_rows(x, g_ref[...]).astype(BF16)
        o_ref[...] = x

    h = h_ref[...]
    a = jnp.dot(h, wg_ref[...], preferred_element_type=F32)
    u = jnp.dot(h, wu_ref[...], preferred_element_type=F32)
    t = (a * jax.nn.sigmoid(a) * (0.5 * u)).astype(BF16)
    o_ref[...] += jnp.dot(t, wd_ref[...], preferred_element_type=F32)


def _ffn(x, g, wg, wu, wd, tm, tf):
    m, d = x.shape
    f = wg.shape[1]
    return pl.pallas_call(
        _ffn_kernel,
        grid=(m // tm, f // tf),
        in_specs=[
            pl.BlockSpec((tm, d), lambda i, j: (i, 0)),
            pl.BlockSpec((1, d), lambda i, j: (0, 0)),
            pl.BlockSpec((d, tf), lambda i, j: (0, j)),
            pl.BlockSpec((d, tf), lambda i, j: (0, j)),
            pl.BlockSpec((tf, d), lambda i, j: (j, 0)),
        ],
        out_specs=pl.BlockSpec((tm, d), lambda i, j: (i, 0)),
        out_shape=jax.ShapeDtypeStruct((m, d), F32),
        scratch_shapes=[pltpu.VMEM((tm, d), BF16)],
        compiler_params=_cparams(("parallel", "arbitrary")),
        name="ffn",
    )(x, g, wg, wu, wd)


def _normmm_kernel(x_ref, g_ref, w_ref, o_ref, h_ref):
    @pl.when(pl.program_id(1) == 0)
    def _():
        h_ref[...] = _rms_rows(x_ref[...], g_ref[...]).astype(BF16)

    o_ref[...] = jnp.dot(h_ref[...], w_ref[...], preferred_element_type=F32).astype(o_ref.dtype)


def _normmm(x, g, w, col_off, n, out_dtype, tm, tn):
    m, d = x.shape
    return pl.pallas_call(
        _normmm_kernel,
        grid=(m // tm, n // tn),
        in_specs=[
            pl.BlockSpec((tm, d), lambda i, j: (i, 0)),
            pl.BlockSpec((1, d), lambda i, j: (0, 0)),
            pl.BlockSpec((d, tn), lambda i, j: (0, col_off + j)),
        ],
        out_specs=pl.BlockSpec((tm, tn), lambda i, j: (i, j)),
        out_shape=jax.ShapeDtypeStruct((m, n), out_dtype),
        scratch_shapes=[pltpu.VMEM((tm, d), BF16)],
        compiler_params=_cparams(("parallel", "arbitrary")),
        name="normmm",
    )(x, g, w)


def _group_ms(x):
    r = lax.broadcasted_iota(jnp.int32, (LANES, LANES), 0)
    c = lax.broadcasted_iota(jnp.int32, (LANES, LANES), 1)
    grp = jnp.where((r >> 6) == (c >> 6), 1.0, 0.0).astype(BF16)
    x2 = x * x
    hi = x2.astype(BF16)
    r1 = x2 - hi.astype(F32)
    mid = r1.astype(BF16)
    lo = (r1 - mid.astype(F32)).astype(BF16)
    s = (jnp.dot(hi, grp, preferred_element_type=F32)
         + jnp.dot(mid, grp, preferred_element_type=F32)
         + jnp.dot(lo, grp, preferred_element_type=F32))
    return s * (1.0 / HALF)


def _norm_rope(x, g, cos, sin):
    y = x * lax.rsqrt(_group_ms(x) + EPS) * g
    lane = lax.broadcasted_iota(jnp.int32, y.shape, 1)
    first = (lane & (HALF - 1)) < (HALF // 2)
    rot = jnp.where(first, -pltpu.roll(y, LANES - HALF // 2, 1), pltpu.roll(y, HALF // 2, 1))
    return y * cos + rot * sin


def _prep_prompt_kernel(q_ref, k_ref, v_ref, cos_ref, sin_ref, gq_ref, gk_ref,
                        qb_ref, kb_ref, vb_ref, kf_ref, vf_ref):
    cos = cos_ref[...]
    sin = sin_ref[...]
    q = _norm_rope(q_ref[...], gq_ref[...], cos, sin) * (HALF ** -0.5)
    k = _norm_rope(k_ref[...], gk_ref[...], cos, sin)
    v = v_ref[...]
    qb_ref[...] = q.astype(BF16)
    kb_ref[...] = k.astype(BF16)
    vb_ref[...] = v.astype(BF16)
    kf_ref[...] = k
    vf_ref[...] = v


def _prep_prompt(z, cos, sin, gq, gk, batch, seq, tm):
    nt = seq // tm
    row = lambda b, i, h: b * nt + i
    hm = pl.BlockSpec((None, None, tm, HEAD_W), lambda b, i, h: (b, h, i, 0))
    flat = pl.BlockSpec((tm, HEAD_W), lambda b, i, h: (row(b, i, h), h))
    vec = pl.BlockSpec((1, HEAD_W), lambda b, i, h: (0, 0))
    tab = pl.BlockSpec((tm, HEAD_W), lambda b, i, h: (i, 0))
    hm_shape = jax.ShapeDtypeStruct((batch, HEADS, seq, HEAD_W), BF16)
    flat_shape = jax.ShapeDtypeStruct((batch * seq, HEADS * HEAD_W), F32)
    return pl.pallas_call(
        _prep_prompt_kernel,
        grid=(batch, nt, HEADS),
        in_specs=[
            pl.BlockSpec((tm, HEAD_W), lambda b, i, h: (row(b, i, h), h)),
            pl.BlockSpec((tm, HEAD_W), lambda b, i, h: (row(b, i, h), HEADS + h)),
            pl.BlockSpec((tm, HEAD_W), lambda b, i, h: (row(b, i, h), 2 * HEADS + h)),
            tab, tab, vec, vec,
        ],
        out_specs=[hm, hm, hm, flat, flat],
        out_shape=[hm_shape, hm_shape, hm_shape, flat_shape, flat_shape],
        compiler_params=_cparams(("parallel", "parallel", "parallel")),
        name="prep_prompt",
    )(z, z, z, cos, sin, gq, gk)


def _prep_sample_kernel(q_ref, k_ref, v_ref, cos_ref, sin_ref, gq_ref, gk_ref,
                        qf_ref, kf_ref, vf_ref):
    cos = cos_ref[...]
    sin = sin_ref[...]
    qf_ref[...] = _norm_rope(q_ref[...], gq_ref[...], cos, sin) * (HALF ** -0.5)
    kf_ref[...] = _norm_rope(k_ref[...], gk_ref[...], cos, sin)
    vf_ref[...] = v_ref[...]


def _prep_sample(z, cos, sin, gq, gk):
    m = z.shape[0]
    blk = lambda off: pl.BlockSpec((m, HEAD_W), lambda h: (0, off + h))
    vec = pl.BlockSpec((1, HEAD_W), lambda h: (0, 0))
    tab = pl.BlockSpec((m, HEAD_W), lambda h: (0, 0))
    shape = jax.ShapeDtypeStruct((m, HEADS * HEAD_W), F32)
    return pl.pallas_call(
        _prep_sample_kernel,
        grid=(HEADS,),
        in_specs=[blk(0), blk(HEADS), blk(2 * HEADS), tab, tab, vec, vec],
        out_specs=[blk(0), blk(0), blk(0)],
        out_shape=[shape, shape, shape],
        compiler_params=_cparams(("parallel",)),
        name="prep_sample",
    )(z, z, z, cos, sin, gq, gk)


def _lambda(lam_ref):
    l = lam_ref[...]
    s1 = jnp.sum(l[0:1] * l[1:2], axis=1, keepdims=True)
    s2 = jnp.sum(l[2:3] * l[3:4], axis=1, keepdims=True)
    return jnp.exp(s1) - jnp.exp(s2) + LAM_INIT


def _pattn_kernel(lam_ref, q_ref, k_ref, v_ref, o_ref, m_ref, l_ref, acc_ref, *, tq):
    qi = pl.program_id(2)
    q = q_ref[...]
    lane = lax.broadcasted_iota(jnp.int32, q.shape, 1)
    zero = jnp.zeros_like(q)
    qs = jnp.concatenate([jnp.where(lane < HALF, q, zero),
                          jnp.where(lane >= HALF, q, zero)], axis=0)

    m_ref[...] = jnp.full(m_ref.shape, NEG, F32)
    l_ref[...] = jnp.zeros(l_ref.shape, F32)
    acc_ref[...] = jnp.zeros(acc_ref.shape, F32)

    def step(kb, masked):
        start = pl.multiple_of(kb * tq, tq)
        k = k_ref[pl.ds(start, tq), :]
        v = v_ref[pl.ds(start, tq), :]
        s = _nt_dot(qs, k)
        if masked:
            r = lax.broadcasted_iota(jnp.int32, s.shape, 0)
            c = lax.broadcasted_iota(jnp.int32, s.shape, 1)
            r = jnp.where(r >= tq, r - tq, r)
            s = jnp.where(r >= c, s, NEG)
        m_prev = m_ref[...]
        m_new = jnp.maximum(m_prev, jnp.max(s, axis=1, keepdims=True))
        alpha = jnp.exp(m_prev - m_new)
        p = jnp.exp(s - jnp.concatenate([m_new] * (tq // LANES), axis=1))
        lsum = p[:, 0:LANES]
        for cblk in range(1, tq // LANES):
            lsum = lsum + p[:, cblk * LANES:(cblk + 1) * LANES]
        l_ref[...] = alpha * l_ref[...] + lsum
        acc_ref[...] = alpha * acc_ref[...] + jnp.dot(p.astype(BF16), v, preferred_element_type=F32)
        m_ref[...] = m_new

    def body(kb, carry):
        step(kb, False)
        return carry

    lax.fori_loop(0, qi, body, 0)
    step(qi, True)

    l = jnp.sum(l_ref[...], axis=1, keepdims=True)
    o = acc_ref[...] / l
    o_ref[...] = o[:tq] - _lambda(lam_ref) * o[tq:]


def _pattn(lam, qb, kb, vb, tq):
    batch, heads, seq, w = qb.shape
    nq = seq // tq
    return pl.pallas_call(
        functools.partial(_pattn_kernel, tq=tq),
        grid=(batch, heads, nq),
        in_specs=[
            pl.BlockSpec(lam.shape, lambda b, h, i: (0, 0)),
            pl.BlockSpec((None, None, tq, w), lambda b, h, i: (b, h, i, 0)),
            pl.BlockSpec((None, None, seq, w), lambda b, h, i: (b, h, 0, 0)),
            pl.BlockSpec((None, None, seq, w), lambda b, h, i: (b, h, 0, 0)),
        ],
        out_specs=pl.BlockSpec((tq, w), lambda b, h, i: (b * nq + i, h)),
        out_shape=jax.ShapeDtypeStruct((batch * seq, heads * w), F32),
        scratch_shapes=[pltpu.VMEM((2 * tq, LANES), F32),
                        pltpu.VMEM((2 * tq, LANES), F32),
                        pltpu.VMEM((2 * tq, w), F32)],
        compiler_params=_cparams(("parallel", "parallel", "arbitrary")),
        name="prompt_attn",
    )(lam, qb, kb, vb)


def _sattn_kernel(pt_ref, lam_ref, q_ref, kn_ref, vn_ref, ck_hbm, cv_hbm, o_ref,
                  kbuf, vbuf, sem, *, n_pages):
    b = pl.program_id(0)
    nb = pl.num_programs(0)
    slot = b % 2
    page_rows = PAGE * HEADS

    def page_copies(seq, slot_):
        cps = []
        for p in range(n_pages):
            pg = pt_ref[seq, p]
            dst = pl.ds(p * page_rows, page_rows)
            cps.append(pltpu.make_async_copy(ck_hbm.at[pg], kbuf.at[slot_, dst, :], sem.at[0, slot_]))
            cps.append(pltpu.make_async_copy(cv_hbm.at[pg], vbuf.at[slot_, dst, :], sem.at[1, slot_]))
        return cps

    @pl.when(b == 0)
    def _():
        for cp in page_copies(0, 0):
            cp.start()

    @pl.when(b + 1 < nb)
    def _():
        for cp in page_copies(b + 1, 1 - slot):
            cp.start()

    for cp in page_copies(b, slot):
        cp.wait()

    q = q_ref[...]
    q2 = jnp.concatenate([q, q], axis=0)
    row = lax.broadcasted_iota(jnp.int32, q2.shape, 0)
    lane = lax.broadcasted_iota(jnp.int32, q2.shape, 1)
    qm = jnp.where((lane >> 6) == (row >> 3), q2, 0.0)

    s = _nt_dot(qm.astype(BF16), kbuf[slot].astype(BF16))
    own = ((lax.broadcasted_iota(jnp.int32, s.shape, 1) & (HEADS - 1))
           == (lax.broadcasted_iota(jnp.int32, s.shape, 0) & (HEADS - 1)))
    s = jnp.where(own, s, NEG)
    kn = kn_ref[...]
    s_new = jnp.sum(qm * jnp.concatenate([kn, kn], axis=0), axis=1, keepdims=True)
    m = jnp.maximum(jnp.max(s, axis=1, keepdims=True), s_new)
    p = jnp.exp(s - m)
    p_new = jnp.exp(s_new - m)
    inv = 1.0 / (jnp.sum(p, axis=1, keepdims=True) + p_new)
    lam = _lambda(lam_ref)
    w = p * inv
    w_new = p_new * inv
    pd = w[:HEADS] - lam * w[HEADS:]
    pd_new = w_new[:HEADS] - lam * w_new[HEADS:]
    o_ref[...] = (jnp.dot(pd.astype(BF16), vbuf[slot].astype(BF16), preferred_element_type=F32)
                  + pd_new * vn_ref[...])


def _sattn(page_table, lam, q, kn, vn, ck, cv):
    nb, n_pages = page_table.shape
    rows = n_pages * PAGE * HEADS
    tile = pl.BlockSpec((None, HEADS, HEAD_W), lambda b, pt: (b, 0, 0))
    grid_spec = pltpu.PrefetchScalarGridSpec(
        num_scalar_prefetch=1,
        grid=(nb,),
        in_specs=[
            pl.BlockSpec(lam.shape, lambda b, pt: (0, 0)),
            tile, tile, tile,
            pl.BlockSpec(memory_space=pl.ANY),
            pl.BlockSpec(memory_space=pl.ANY),
        ],
        out_specs=tile,
        scratch_shapes=[pltpu.VMEM((2, rows, HEAD_W), F32),
                        pltpu.VMEM((2, rows, HEAD_W), F32),
                        pltpu.SemaphoreType.DMA((2, 2))],
    )
    return pl.pallas_call(
        functools.partial(_sattn_kernel, n_pages=n_pages),
        grid_spec=grid_spec,
        out_shape=jax.ShapeDtypeStruct((nb, HEADS, HEAD_W), F32),
        compiler_params=_cparams(("arbitrary",)),
        name="sample_attn",
    )(page_table, lam, q, kn, vn, ck, cv)


def _lower_bound(lb_ref, sl):
    a = lb_ref[:, sl]
    mx = jnp.maximum(a[0:1], a[1:2])
    e0 = jnp.exp(a[0:1] - mx)
    e1 = jnp.exp(a[1:2] - mx)
    return e0 / (e0 + e1)


def _split3(x):
    hi = x.astype(BF16)
    r1 = x - hi.astype(F32)
    mid = r1.astype(BF16)
    lo = (r1 - mid.astype(F32)).astype(BF16)
    return hi, mid, lo


def _pair_ref(b, m, c):
    parts = [jnp.broadcast_to(b[2 * m * p + m - 1:2 * m * p + m, :], (2 * m, b.shape[1]))
             for p in range(c // (2 * m))]
    return parts[0] if len(parts) == 1 else jnp.concatenate(parts, axis=0)


def _hgrn_prompt_kernel(lb_ref, q_ref, f_ref, i_ref, o_ref, s_out_ref, st_ref, *, c):
    ci = pl.program_id(1)

    @pl.when(ci == 0)
    def _():
        st_ref[...] = jnp.zeros(st_ref.shape, F32)

    row = lax.broadcasted_iota(jnp.int32, (c, c), 0)
    col = lax.broadcasted_iota(jnp.int32, (c, c), 1)
    tri = jnp.where(col <= row, 1.0, 0.0).astype(BF16)
    same_sub = (row >> 4) == (col >> 4)
    causal = col <= row
    mask16 = (row >> 5) == (col >> 5)
    mask32 = (row >> 6) == (col >> 6)
    trow = lax.broadcasted_iota(jnp.int32, (c, HEAD_W), 0)
    sub = 16

    for h in range(HEADS):
        sl = slice(h * HEAD_W, (h + 1) * HEAD_W)
        lb = _lower_bound(lb_ref, sl)
        q = q_ref[:, sl]
        v = i_ref[:, sl]
        f = lb + (1.0 - lb) * jax.nn.sigmoid(f_ref[:, sl])
        k = 1.0 - f
        hi, mid, lo = _split3(jnp.log(f))
        b = (jnp.dot(tri, hi, preferred_element_type=F32)
             + jnp.dot(tri, mid, preferred_element_type=F32)
             + jnp.dot(tri, lo, preferred_element_type=F32))

        ref_d = jnp.concatenate(
            [jnp.zeros((sub, HEAD_W), F32)]
            + [jnp.broadcast_to(b[sub * j - 1:sub * j, :], (sub, HEAD_W)) for j in range(1, c // sub)],
            axis=0)
        arg_d = b - ref_d
        p_d = _nt_dot((q * jnp.exp(arg_d)).astype(BF16), (k * jnp.exp(-arg_d)).astype(BF16))
        att = jnp.where(same_sub, jnp.where(causal, p_d, 0.0), 0.0)
        for m, msk in ((16, mask16), (32, mask32), (64, None)):
            odd = ((trow >> int(math.log2(m))) & 1) == 1
            d = b - _pair_ref(b, m, c)
            e = jnp.exp(jnp.where(odd, d, -d))
            qh = jnp.where(odd, q * e, 0.0).astype(BF16)
            kh = jnp.where(odd, 0.0, k * e).astype(BF16)
            p_m = _nt_dot(qh, kh)
            att = att + (p_m if msk is None else jnp.where(msk, p_m, 0.0))

        b_last = b[c - 1:c, :]
        st = st_ref[h]
        o = (jnp.dot(att.astype(BF16), v.astype(BF16), preferred_element_type=F32)
             + _nt_dot((q * jnp.exp(b)).astype(BF16), st.astype(BF16)))
        o_ref[:, sl] = o
        kt = (k * jnp.exp(b_last - b)).astype(BF16)
        st_new = st * jnp.exp(b_last) + jnp.dot(v.T.astype(BF16), kt, preferred_element_type=F32)
        st_ref[h] = st_new

        @pl.when(ci == pl.num_programs(1) - 1)
        def _():
            s_out_ref[h] = st_new.T


def _hgrn_prompt(lb2, z, batch, seq, c):
    nc = seq // c
    width = HEADS * HEAD_W
    col = lambda off: pl.BlockSpec((c, width), lambda b, i: (b * nc + i, off))
    return pl.pallas_call(
        functools.partial(_hgrn_prompt_kernel, c=c),
        grid=(batch, nc),
        in_specs=[pl.BlockSpec(lb2.shape, lambda b, i: (0, 0)), col(3), col(4), col(5)],
        out_specs=[pl.BlockSpec((c, width), lambda b, i: (b * nc + i, 0)),
                   pl.BlockSpec((None, HEADS, HEAD_W, HEAD_W), lambda b, i: (b, 0, 0, 0))],
        out_shape=[jax.ShapeDtypeStruct((batch * seq, width), F32),
                   jax.ShapeDtypeStruct((batch, HEADS, HEAD_W, HEAD_W), F32)],
        scratch_shapes=[pltpu.VMEM((HEADS, HEAD_W, HEAD_W), F32)],
        compiler_params=_cparams(("parallel", "arbitrary")),
        name="hgrn_prompt",
    )(lb2, z, z, z)


def _hgrn_sample_kernel(lb_ref, q_ref, f_ref, i_ref, s_ref, o_ref, s_out_ref, *, rows):
    r_i = lax.broadcasted_iota(jnp.int32, (HEAD_W, HEAD_W), 0)
    c_i = lax.broadcasted_iota(jnp.int32, (HEAD_W, HEAD_W), 1)
    eye = r_i == c_i

    def to_col(x):
        return jnp.sum(jnp.where(eye, x, 0.0), axis=1, keepdims=True)

    for h in range(HEADS):
        sl = slice(h * HEAD_W, (h + 1) * HEAD_W)
        lb = _lower_bound(lb_ref, sl)
        f_all = lb + (1.0 - lb) * jax.nn.sigmoid(f_ref[:, sl])
        q_all = q_ref[:, sl]
        i_all = i_ref[:, sl]
        o_rows = []
        for r in range(rows):
            f_col = to_col(f_all[r:r + 1])
            q_col = to_col(q_all[r:r + 1])
            s_new = f_col * s_ref[r, h] + (1.0 - f_col) * i_all[r:r + 1]
            s_out_ref[r, h] = s_new
            o_rows.append(jnp.sum(q_col * s_new, axis=0, keepdims=True))
        o_ref[:, sl] = jnp.concatenate(o_rows, axis=0)


def _hgrn_sample(lb2, z, state, rows):
    nb = state.shape[0]
    width = HEADS * HEAD_W
    col = lambda off: pl.BlockSpec((rows, width), lambda i: (i, off))
    sspec = pl.BlockSpec((rows, HEADS, HEAD_W, HEAD_W), lambda i: (i, 0, 0, 0))
    return pl.pallas_call(
        functools.partial(_hgrn_sample_kernel, rows=rows),
        grid=(nb // rows,),
        in_specs=[pl.BlockSpec(lb2.shape, lambda i: (0, 0)), col(3), col(4), col(5), sspec],
        out_specs=[pl.BlockSpec((rows, width), lambda i: (i, 0)), sspec],
        out_shape=[jax.ShapeDtypeStruct((nb, width), F32),
                   jax.ShapeDtypeStruct(state.shape, F32)],
        compiler_params=_cparams(("parallel",)),
        name="hgrn_sample",
    )(lb2, z, z, z, state)


def _head_rms(x, g):
    parts = []
    for h in range(HEADS):
        blk = x[:, h * HEAD_W:(h + 1) * HEAD_W]
        parts.append(_rms_rows(blk, g))
    return jnp.concatenate(parts, axis=1)


def _merge_kernel(oa_ref, ob_ref, og_ref, ga_ref, gb_ref, na_ref, nb_ref, wa_ref, wb_ref,
                  o_ref, a_ref, b_ref):
    @pl.when(pl.program_id(1) == 0)
    def _():
        a_ref[...] = (_head_rms(oa_ref[...], na_ref[...]) * (1.0 - LAM_INIT)).astype(BF16)
        og = og_ref[...].astype(F32)
        b_ref[...] = (_head_rms(ob_ref[...], nb_ref[...]) * (og * jax.nn.sigmoid(og))).astype(BF16)

    pa = jnp.dot(a_ref[...], wa_ref[...], preferred_element_type=F32)
    pb = jnp.dot(b_ref[...], wb_ref[...], preferred_element_type=F32)
    o_ref[...] = (jax.nn.sigmoid(ga_ref[...].astype(F32)) * pa
                  + jax.nn.sigmoid(gb_ref[...].astype(F32)) * pb).astype(o_ref.dtype)


def _merge(oa, ob, zg, na, nb, wa, wb, tm, tn):
    m, w = oa.shape
    n = wa.shape[1]
    goff = w // tn
    return pl.pallas_call(
        _merge_kernel,
        grid=(m // tm, n // tn),
        in_specs=[
            pl.BlockSpec((tm, w), lambda i, j: (i, 0)),
            pl.BlockSpec((tm, w), lambda i, j: (i, 0)),
            pl.BlockSpec((tm, w), lambda i, j: (i, 0)),
            pl.BlockSpec((tm, tn), lambda i, j: (i, goff + j)),
            pl.BlockSpec((tm, tn), lambda i, j: (i, goff + n // tn + j)),
            pl.BlockSpec((1, HEAD_W), lambda i, j: (0, 0)),
            pl.BlockSpec((1, HEAD_W), lambda i, j: (0, 0)),
            pl.BlockSpec((w, tn), lambda i, j: (0, j)),
            pl.BlockSpec((w, tn), lambda i, j: (0, j)),
        ],
        out_specs=pl.BlockSpec((tm, tn), lambda i, j: (i, j)),
        out_shape=jax.ShapeDtypeStruct((m, n), BF16),
        scratch_shapes=[pltpu.VMEM((tm, w), BF16), pltpu.VMEM((tm, w), BF16)],
        compiler_params=_cparams(("parallel", "arbitrary")),
        name="merge",
    )(oa, ob, zg, zg, zg, na, nb, wa, wb)


def _resmm_kernel(x_ref, a_ref, w_ref, o_ref):
    o_ref[...] = x_ref[...] + jnp.dot(a_ref[...], w_ref[...], preferred_element_type=F32)


def _resmm(x, a, w, tm, tn):
    m, kdim = a.shape
    n = w.shape[1]
    return pl.pallas_call(
        _resmm_kernel,
        grid=(m // tm, n // tn),
        in_specs=[
            pl.BlockSpec((tm, tn), lambda i, j: (i, j)),
            pl.BlockSpec((tm, kdim), lambda i, j: (i, 0)),
            pl.BlockSpec((kdim, tn), lambda i, j: (0, j)),
        ],
        out_specs=pl.BlockSpec((tm, tn), lambda i, j: (i, j)),
        out_shape=jax.ShapeDtypeStruct((m, n), F32),
        compiler_params=_cparams(("parallel", "parallel")),
        name="resmm",
    )(x, a, w)


def _rope_tables(pos):
    half = HALF // 2
    inv = ROPE_THETA ** (-jnp.arange(half, dtype=F32) / half)
    ang = pos.astype(F32)[:, None] * inv[None, :]
    cos = jnp.tile(jnp.cos(ang), (1, LANES // half))
    sin = jnp.tile(jnp.sin(ang), (1, LANES // half))
    return cos, sin


def kernel(x_prompt, x_sample, cache_k, cache_v, state_hgrn, page_table, ffn1_norm, ffn1_w_gate, ffn1_w_up, ffn1_w_down, mix_norm, w_in, q_norm, k_norm, lambda_q1, lambda_k1, lambda_q2, lambda_k2, attn_sub_norm, hgrn_lower_bounds, hgrn_out_norm, w_proj_a, w_proj_b, w_out, ffn2_norm, ffn2_w_gate, ffn2_w_up, ffn2_w_down):
    batch, seq, d = x_prompt.shape
    nb = x_sample.shape[0]
    n_pages = page_table.shape[1]
    width = HEADS * HEAD_W
    n_a = 6 * width

    bf = lambda a: a.astype(BF16)
    w1g, w1u, w1d = bf(ffn1_w_gate[0]), bf(ffn1_w_up[0]), bf(ffn1_w_down[0])
    w2g, w2u, w2d = bf(ffn2_w_gate[0]), bf(ffn2_w_up[0]), bf(ffn2_w_down[0])
    w_in_b = bf(w_in[0])
    n_g = w_in_b.shape[1] - n_a
    wa, wb, wo = bf(w_proj_a[0]), bf(w_proj_b[0]), bf(w_out[0])
    lam = jnp.concatenate([lambda_q1, lambda_k1, lambda_q2, lambda_k2], axis=0).astype(F32)
    gq = jnp.tile(q_norm.astype(F32), (1, 2))
    gk = jnp.tile(k_norm.astype(F32), (1, 2))
    lb2 = hgrn_lower_bounds.astype(F32)

    def front(x, tm_ffn, tm_in):
        x1 = _ffn(x, ffn1_norm, w1g, w1u, w1d, tm_ffn, 512)
        za = _normmm(x1, mix_norm, w_in_b, 0, n_a, F32, tm_in, 512)
        zg = _normmm(x1, mix_norm, w_in_b, n_a // 512, n_g, BF16, tm_in, 512)
        return x1, za, zg

    def back(x1, oa, ob, zg, tm):
        mg = _merge(oa, ob, zg, attn_sub_norm, hgrn_out_norm, wa, wb, tm, 512)
        x2 = _resmm(x1, mg, wo, tm, 512)
        return _ffn(x2, ffn2_norm, w2g, w2u, w2d, tm, 512)

    xp = x_prompt.reshape(batch * seq, d)
    x1p, zap, zgp = front(xp, 512, 1024)
    cos_p, sin_p = _rope_tables(jnp.arange(seq))
    qb, kb, vb, kf, vf = _prep_prompt(zap, cos_p, sin_p, gq, gk, batch, seq, 512)
    oa_p = _pattn(lam, qb, kb, vb, 512)
    ob_p, st_p = _hgrn_prompt(lb2, zap, batch, seq, 128)
    y_p = back(x1p, oa_p, ob_p, zgp, 512)

    xs = x_sample.reshape(nb, d)
    x1s, zas, zgs = front(xs, nb, nb)
    pos_s = jnp.full((nb,), n_pages * PAGE, jnp.int32)
    cos_s, sin_s = _rope_tables(pos_s)
    qs, ks, vs = _prep_sample(zas, cos_s, sin_s, gq, gk)
    ck = cache_k.reshape(cache_k.shape[1], PAGE * HEADS, HEAD_W)
    cv = cache_v.reshape(cache_v.shape[1], PAGE * HEADS, HEAD_W)
    tiles = lambda a: a.reshape(nb, HEADS, HEAD_W)
    oa_s = _sattn(page_table, lam, tiles(qs), tiles(ks), tiles(vs), ck, cv).reshape(nb, width)
    ob_s, st_s = _hgrn_sample(lb2, zas, state_hgrn[0], 8)
    y_s = back(x1s, oa_s, ob_s, zgs, nb)

    return (y_p.reshape(batch, seq, d),
            y_s.reshape(nb, 1, d),
            kf.reshape(1, batch, seq, HEADS, HEAD_W),
            vf.reshape(1, batch, seq, HEADS, HEAD_W),
            ks.reshape(1, nb, 1, HEADS, HEAD_W),
            vs.reshape(1, nb, 1, HEADS, HEAD_W),
            st_p.reshape(1, batch, HEADS, HEAD_W, HEAD_W),
            st_s.reshape(1, nb, HEADS, HEAD_W, HEAD_W))
```

```python
import functools
import math

import jax
import jax.numpy as jnp
from jax import lax
from jax.experimental import pallas as pl
from jax.experimental.pallas import tpu as pltpu

F32 = jnp.float32
BF16 = jnp.bfloat16

EPS = 1e-6
ROPE_THETA = 10000.0
HEADS = 8
HEAD_W = 128
HALF = 64
PAGE = 128
LAM_INIT = 0.8 - 0.6 * math.exp(-0.3 * 0)
LANES = 128
VMEM_LIMIT = 56 * 1024 * 1024
NEG = -1e30


def _cparams(sem):
    return pltpu.CompilerParams(dimension_semantics=sem, vmem_limit_bytes=VMEM_LIMIT)


def _nt_dot(a, b):
    return lax.dot_general(a, b, (((1,), (1,)), ((), ())), preferred_element_type=F32)


def _rms_rows(x, g):
    ms = jnp.mean(x * x, axis=-1, keepdims=True)
    return x * lax.rsqrt(ms + EPS) * g


def _ffn_kernel(x_ref, g_ref, wg_ref, wu_ref, wd_ref, o_ref, h_ref):
    j = pl.program_id(1)

    @pl.when(j == 0)
    def _():
        x = x_ref[...]
        h_ref[...] = _rms_rows(x, g_ref[...]).astype(BF16)
        o_ref[...] = x

    h = h_ref[...]
    a = jnp.dot(h, wg_ref[...].astype(BF16), preferred_element_type=F32)
    u = jnp.dot(h, wu_ref[...].astype(BF16), preferred_element_type=F32)
    t = (a * jax.nn.sigmoid(a) * (0.5 * u)).astype(BF16)
    o_ref[...] += jnp.dot(t, wd_ref[...].astype(BF16), preferred_element_type=F32)


def _ffn(x, g, wg, wu, wd, tm, tf):
    m, d = x.shape
    f = wg.shape[1]
    return pl.pallas_call(
        _ffn_kernel,
        grid=(m // tm, f // tf),
        in_specs=[
            pl.BlockSpec((tm, d), lambda i, j: (i, 0)),
            pl.BlockSpec((1, d), lambda i, j: (0, 0)),
            pl.BlockSpec((d, tf), lambda i, j: (0, j)),
            pl.BlockSpec((d, tf), lambda i, j: (0, j)),
            pl.BlockSpec((tf, d), lambda i, j: (j, 0)),
        ],
        out_specs=pl.BlockSpec((tm, d), lambda i, j: (i, 0), pipeline_mode=pl.Buffered(1)),
        out_shape=jax.ShapeDtypeStruct((m, d), F32),
        scratch_shapes=[pltpu.VMEM((tm, d), BF16)],
        compiler_params=_cparams(("parallel", "arbitrary")),
        name="ffn",
    )(x, g, wg, wu, wd)


def _normmm_kernel(x_ref, g_ref, w_ref, o_ref, h_ref):
    @pl.when(pl.program_id(1) == 0)
    def _():
        h_ref[...] = _rms_rows(x_ref[...], g_ref[...]).astype(BF16)

    o_ref[...] = jnp.dot(h_ref[...], w_ref[...].astype(BF16), preferred_element_type=F32).astype(o_ref.dtype)


def _normmm(x, g, w, col_off, n, out_dtype, tm, tn):
    m, d = x.shape
    return pl.pallas_call(
        _normmm_kernel,
        grid=(m // tm, n // tn),
        in_specs=[
            pl.BlockSpec((tm, d), lambda i, j: (i, 0)),
            pl.BlockSpec((1, d), lambda i, j: (0, 0)),
            pl.BlockSpec((d, tn), lambda i, j: (0, col_off + j)),
        ],
        out_specs=pl.BlockSpec((tm, tn), lambda i, j: (i, j)),
        out_shape=jax.ShapeDtypeStruct((m, n), out_dtype),
        scratch_shapes=[pltpu.VMEM((tm, d), BF16)],
        compiler_params=_cparams(("parallel", "arbitrary")),
        name="normmm",
    )(x, g, w)


def _group_ms(x):
    r = lax.broadcasted_iota(jnp.int32, (LANES, LANES), 0)
    c = lax.broadcasted_iota(jnp.int32, (LANES, LANES), 1)
    grp = jnp.where((r >> 6) == (c >> 6), 1.0, 0.0).astype(BF16)
    x2 = x * x
    hi = x2.astype(BF16)
    r1 = x2 - hi.astype(F32)
    mid = r1.astype(BF16)
    lo = (r1 - mid.astype(F32)).astype(BF16)
    s = (jnp.dot(hi, grp, preferred_element_type=F32)
         + jnp.dot(mid, grp, preferred_element_type=F32)
         + jnp.dot(lo, grp, preferred_element_type=F32))
    return s * (1.0 / HALF)


def _norm_rope(x, g, cos, sin):
    y = x * lax.rsqrt(_group_ms(x) + EPS) * g
    lane = lax.broadcasted_iota(jnp.int32, y.shape, 1)
    first = (lane & (HALF - 1)) < (HALF // 2)
    rot = jnp.where(first, -pltpu.roll(y, LANES - HALF // 2, 1), pltpu.roll(y, HALF // 2, 1))
    return y * cos + rot * sin


def _prep_prompt_kernel(q_ref, k_ref, v_ref, cos_ref, sin_ref, gq_ref, gk_ref,
                        qb_ref, kb_ref, vb_ref, kf_ref, vf_ref):
    cos = cos_ref[...]
    sin = sin_ref[...]
    gq = gq_ref[...]
    gk = gk_ref[...]
    vf_ref[...] = v_ref[...]
    for h in range(HEADS):
        sl = slice(h * HEAD_W, (h + 1) * HEAD_W)
        k = _norm_rope(k_ref[:, sl], gk, cos, sin)
        kf_ref[:, sl] = k
        kb_ref[h] = k.astype(BF16)
        qb_ref[h] = (_norm_rope(q_ref[:, sl], gq, cos, sin) * (HALF ** -0.5)).astype(BF16)
        vb_ref[h] = v_ref[:, sl].astype(BF16)


def _prep_prompt(z, cos, sin, gq, gk, batch, seq, tm):
    nt = seq // tm
    width = HEADS * HEAD_W
    row = lambda b, i: b * nt + i
    hm = pl.BlockSpec((None, HEADS, tm, HEAD_W), lambda b, i: (b, 0, i, 0))
    flat = pl.BlockSpec((tm, width), lambda b, i: (row(b, i), 0))
    vec = pl.BlockSpec((1, HEAD_W), lambda b, i: (0, 0))
    tab = pl.BlockSpec((tm, HEAD_W), lambda b, i: (i, 0))
    hm_shape = jax.ShapeDtypeStruct((batch, HEADS, seq, HEAD_W), BF16)
    flat_shape = jax.ShapeDtypeStruct((batch * seq, width), F32)
    return pl.pallas_call(
        _prep_prompt_kernel,
        grid=(batch, nt),
        in_specs=[
            pl.BlockSpec((tm, width), lambda b, i: (row(b, i), 0)),
            pl.BlockSpec((tm, width), lambda b, i: (row(b, i), 1)),
            pl.BlockSpec((tm, width), lambda b, i: (row(b, i), 2)),
            tab, tab, vec, vec,
        ],
        out_specs=[hm, hm, hm, flat, flat],
        out_shape=[hm_shape, hm_shape, hm_shape, flat_shape, flat_shape],
        compiler_params=_cparams(("parallel", "parallel")),
        name="prep_prompt",
    )(z, z, z, cos, sin, gq, gk)


def _prep_sample_kernel(q_ref, k_ref, v_ref, cos_ref, sin_ref, gq_ref, gk_ref,
                        qf_ref, kf_ref, vf_ref):
    cos = cos_ref[...]
    sin = sin_ref[...]
    qf_ref[...] = _norm_rope(q_ref[...], gq_ref[...], cos, sin) * (HALF ** -0.5)
    kf_ref[...] = _norm_rope(k_ref[...], gk_ref[...], cos, sin)
    vf_ref[...] = v_ref[...]


def _prep_sample(z, cos, sin, gq, gk):
    m = z.shape[0]
    blk = lambda off: pl.BlockSpec((m, HEAD_W), lambda h: (0, off + h))
    vec = pl.BlockSpec((1, HEAD_W), lambda h: (0, 0))
    tab = pl.BlockSpec((m, HEAD_W), lambda h: (0, 0))
    shape = jax.ShapeDtypeStruct((m, HEADS * HEAD_W), F32)
    return pl.pallas_call(
        _prep_sample_kernel,
        grid=(HEADS,),
        in_specs=[blk(0), blk(HEADS), blk(2 * HEADS), tab, tab, vec, vec],
        out_specs=[blk(0), blk(0), blk(0)],
        out_shape=[shape, shape, shape],
        compiler_params=_cparams(("parallel",)),
        name="prep_sample",
    )(z, z, z, cos, sin, gq, gk)


def _lambda(lam_ref):
    l = lam_ref[...]
    s1 = jnp.sum(l[0:1] * l[1:2], axis=1, keepdims=True)
    s2 = jnp.sum(l[2:3] * l[3:4], axis=1, keepdims=True)
    return jnp.exp(s1) - jnp.exp(s2) + LAM_INIT


def _pattn_kernel(lam_ref, q_ref, k_ref, v_ref, o_ref, m_ref, l_ref, acc_ref, *, tq):
    qi = pl.program_id(2)
    q = q_ref[...]
    lane = lax.broadcasted_iota(jnp.int32, q.shape, 1)
    zero = jnp.zeros_like(q)
    qs = jnp.concatenate([jnp.where(lane < HALF, q, zero),
                          jnp.where(lane >= HALF, q, zero)], axis=0)

    m_ref[...] = jnp.full(m_ref.shape, NEG, F32)
    l_ref[...] = jnp.zeros(l_ref.shape, F32)
    acc_ref[...] = jnp.zeros(acc_ref.shape, F32)

    def step(kb, masked):
        start = pl.multiple_of(kb * tq, tq)
        k = k_ref[pl.ds(start, tq), :]
        v = v_ref[pl.ds(start, tq), :]
        s = _nt_dot(qs, k)
        if masked:
            r = lax.broadcasted_iota(jnp.int32, s.shape, 0)
            c = lax.broadcasted_iota(jnp.int32, s.shape, 1)
            r = jnp.where(r >= tq, r - tq, r)
            s = jnp.where(r >= c, s, NEG)
        m_prev = m_ref[...]
        m_new = jnp.maximum(m_prev, jnp.max(s, axis=1, keepdims=True))
        alpha = jnp.exp(m_prev - m_new)
        p = jnp.exp(s - jnp.concatenate([m_new] * (tq // LANES), axis=1))
        lsum = p[:, 0:LANES]
        for cblk in range(1, tq // LANES):
            lsum = lsum + p[:, cblk * LANES:(cblk + 1) * LANES]
        l_ref[...] = alpha * l_ref[...] + lsum
        acc_ref[...] = alpha * acc_ref[...] + jnp.dot(p.astype(BF16), v, preferred_element_type=F32)
        m_ref[...] = m_new

    def body(kb, carry):
        step(kb, False)
        return carry

    lax.fori_loop(0, qi, body, 0)
    step(qi, True)

    l = jnp.sum(l_ref[...], axis=1, keepdims=True)
    o = acc_ref[...] / l
    o_ref[...] = o[:tq] - _lambda(lam_ref) * o[tq:]


def _pattn(lam, qb, kb, vb, tq):
    batch, heads, seq, w = qb.shape
    nq = seq // tq
    return pl.pallas_call(
        functools.partial(_pattn_kernel, tq=tq),
        grid=(batch, heads, nq),
        in_specs=[
            pl.BlockSpec(lam.shape, lambda b, h, i: (0, 0)),
            pl.BlockSpec((None, None, tq, w), lambda b, h, i: (b, h, i, 0)),
            pl.BlockSpec((None, None, seq, w), lambda b, h, i: (b, h, 0, 0)),
            pl.BlockSpec((None, None, seq, w), lambda b, h, i: (b, h, 0, 0)),
        ],
        out_specs=pl.BlockSpec((tq, w), lambda b, h, i: (b * nq + i, h)),
        out_shape=jax.ShapeDtypeStruct((batch * seq, heads * w), F32),
        scratch_shapes=[pltpu.VMEM((2 * tq, LANES), F32),
                        pltpu.VMEM((2 * tq, LANES), F32),
                        pltpu.VMEM((2 * tq, w), F32)],
        compiler_params=_cparams(("parallel", "parallel", "arbitrary")),
        name="prompt_attn",
    )(lam, qb, kb, vb)


def _sattn_kernel(pt_ref, lam_ref, q_ref, kn_ref, vn_ref, ck_hbm, cv_hbm, o_ref,
                  kbuf, vbuf, sem, *, n_pages):
    b = pl.program_id(0)
    nb = pl.num_programs(0)
    slot = b % 2
    page_rows = PAGE * HEADS

    def page_copies(seq, slot_):
        cps = []
        for p in range(n_pages):
            pg = pt_ref[seq, p]
            dst = pl.ds(p * page_rows, page_rows)
            cps.append(pltpu.make_async_copy(ck_hbm.at[pg], kbuf.at[slot_, dst, :], sem.at[0, slot_]))
            cps.append(pltpu.make_async_copy(cv_hbm.at[pg], vbuf.at[slot_, dst, :], sem.at[1, slot_]))
        return cps

    @pl.when(b == 0)
    def _():
        for cp in page_copies(0, 0):
            cp.start()

    @pl.when(b + 1 < nb)
    def _():
        for cp in page_copies(b + 1, 1 - slot):
            cp.start()

    for cp in page_copies(b, slot):
        cp.wait()

    q = q_ref[...]
    q2 = jnp.concatenate([q, q], axis=0)
    row = lax.broadcasted_iota(jnp.int32, q2.shape, 0)
    lane = lax.broadcasted_iota(jnp.int32, q2.shape, 1)
    qm = jnp.where((lane >> 6) == (row >> 3), q2, 0.0)

    s = _nt_dot(qm.astype(BF16), kbuf[slot].astype(BF16))
    own = ((lax.broadcasted_iota(jnp.int32, s.shape, 1) & (HEADS - 1))
           == (lax.broadcasted_iota(jnp.int32, s.shape, 0) & (HEADS - 1)))
    s = jnp.where(own, s, NEG)
    kn = kn_ref[...]
    s_new = jnp.sum(qm * jnp.concatenate([kn, kn], axis=0), axis=1, keepdims=True)
    m = jnp.maximum(jnp.max(s, axis=1, keepdims=True), s_new)
    p = jnp.exp(s - m)
    p_new = jnp.exp(s_new - m)
    inv = 1.0 / (jnp.sum(p, axis=1, keepdims=True) + p_new)
    lam = _lambda(lam_ref)
    w = p * inv
    w_new = p_new * inv
    pd = w[:HEADS] - lam * w[HEADS:]
    pd_new = w_new[:HEADS] - lam * w_new[HEADS:]
    o_ref[...] = (jnp.dot(pd.astype(BF16), vbuf[slot].astype(BF16), preferred_element_type=F32)
                  + pd_new * vn_ref[...])


def _sattn(page_table, lam, q, kn, vn, ck, cv):
    nb, n_pages = page_table.shape
    rows = n_pages * PAGE * HEADS
    tile = pl.BlockSpec((None, HEADS, HEAD_W), lambda b, pt: (b, 0, 0))
    grid_spec = pltpu.PrefetchScalarGridSpec(
        num_scalar_prefetch=1,
        grid=(nb,),
        in_specs=[
            pl.BlockSpec(lam.shape, lambda b, pt: (0, 0)),
            tile, tile, tile,
            pl.BlockSpec(memory_space=pl.ANY),
            pl.BlockSpec(memory_space=pl.ANY),
        ],
        out_specs=tile,
        scratch_shapes=[pltpu.VMEM((2, rows, HEAD_W), F32),
                        pltpu.VMEM((2, rows, HEAD_W), F32),
                        pltpu.SemaphoreType.DMA((2, 2))],
    )
    return pl.pallas_call(
        functools.partial(_sattn_kernel, n_pages=n_pages),
        grid_spec=grid_spec,
        out_shape=jax.ShapeDtypeStruct((nb, HEADS, HEAD_W), F32),
        compiler_params=_cparams(("arbitrary",)),
        name="sample_attn",
    )(page_table, lam, q, kn, vn, ck, cv)


def _lower_bound(lb_ref, sl):
    a = lb_ref[:, sl]
    mx = jnp.maximum(a[0:1], a[1:2])
    e0 = jnp.exp(a[0:1] - mx)
    e1 = jnp.exp(a[1:2] - mx)
    return e0 / (e0 + e1)


def _split3(x):
    hi = x.astype(BF16)
    r1 = x - hi.astype(F32)
    mid = r1.astype(BF16)
    lo = (r1 - mid.astype(F32)).astype(BF16)
    return hi, mid, lo


def _pair_ref(b, m, c):
    parts = [jnp.broadcast_to(b[2 * m * p + m - 1:2 * m * p + m, :], (2 * m, b.shape[1]))
             for p in range(c // (2 * m))]
    return parts[0] if len(parts) == 1 else jnp.concatenate(parts, axis=0)


def _hgrn_prompt_kernel(lb_ref, q_ref, f_ref, i_ref, o_ref, s_out_ref, st_ref, *, c):
    ci = pl.program_id(1)

    @pl.when(ci == 0)
    def _():
        st_ref[...] = jnp.zeros(st_ref.shape, F32)

    row = lax.broadcasted_iota(jnp.int32, (c, c), 0)
    col = lax.broadcasted_iota(jnp.int32, (c, c), 1)
    tri = jnp.where(col <= row, 1.0, 0.0).astype(BF16)
    same_sub = (row >> 4) == (col >> 4)
    causal = col <= row
    mask16 = (row >> 5) == (col >> 5)
    mask32 = (row >> 6) == (col >> 6)
    trow = lax.broadcasted_iota(jnp.int32, (c, HEAD_W), 0)
    sub = 16

    for h in range(HEADS):
        sl = slice(h * HEAD_W, (h + 1) * HEAD_W)
        lb = _lower_bound(lb_ref, sl)
        q = q_ref[:, sl]
        v = i_ref[:, sl]
        f = lb + (1.0 - lb) * jax.nn.sigmoid(f_ref[:, sl])
        k = 1.0 - f
        hi, mid, lo = _split3(jnp.log(f))
        b = (jnp.dot(tri, hi, preferred_element_type=F32)
             + jnp.dot(tri, mid, preferred_element_type=F32)
             + jnp.dot(tri, lo, preferred_element_type=F32))

        ref_d = jnp.concatenate(
            [jnp.zeros((sub, HEAD_W), F32)]
            + [jnp.broadcast_to(b[sub * j - 1:sub * j, :], (sub, HEAD_W)) for j in range(1, c // sub)],
            axis=0)
        arg_d = b - ref_d
        p_d = _nt_dot((q * jnp.exp(arg_d)).astype(BF16), (k * jnp.exp(-arg_d)).astype(BF16))
        att = jnp.where(same_sub, jnp.where(causal, p_d, 0.0), 0.0)
        for m, msk in ((16, mask16), (32, mask32), (64, None)):
            odd = ((trow >> int(math.log2(m))) & 1) == 1
            d = b - _pair_ref(b, m, c)
            e = jnp.exp(jnp.where(odd, d, -d))
            qh = jnp.where(odd, q * e, 0.0).astype(BF16)
            kh = jnp.where(odd, 0.0, k * e).astype(BF16)
            p_m = _nt_dot(qh, kh)
            att = att + (p_m if msk is None else jnp.where(msk, p_m, 0.0))

        b_last = b[c - 1:c, :]
        st = st_ref[h]
        o = (jnp.dot(att.astype(BF16), v.astype(BF16), preferred_element_type=F32)
             + _nt_dot((q * jnp.exp(b)).astype(BF16), st.astype(BF16)))
        o_ref[:, sl] = o
        kt = (k * jnp.exp(b_last - b)).astype(BF16)
        st_new = st * jnp.exp(b_last) + jnp.dot(v.T.astype(BF16), kt, preferred_element_type=F32)
        st_ref[h] = st_new

    @pl.when(ci == pl.num_programs(1) - 1)
    def _():
        for h in range(HEADS):
            s_out_ref[h] = st_ref[h].T


def _hgrn_prompt(lb2, z, batch, seq, c):
    nc = seq // c
    width = HEADS * HEAD_W
    col = lambda off: pl.BlockSpec((c, width), lambda b, i: (b * nc + i, off))
    return pl.pallas_call(
        functools.partial(_hgrn_prompt_kernel, c=c),
        grid=(batch, nc),
        in_specs=[pl.BlockSpec(lb2.shape, lambda b, i: (0, 0)), col(3), col(4), col(5)],
        out_specs=[pl.BlockSpec((c, width), lambda b, i: (b * nc + i, 0)),
                   pl.BlockSpec((None, HEADS, HEAD_W, HEAD_W), lambda b, i: (b, 0, 0, 0))],
        out_shape=[jax.ShapeDtypeStruct((batch * seq, width), F32),
                   jax.ShapeDtypeStruct((batch, HEADS, HEAD_W, HEAD_W), F32)],
        scratch_shapes=[pltpu.VMEM((HEADS, HEAD_W, HEAD_W), F32)],
        compiler_params=_cparams(("parallel", "arbitrary")),
        name="hgrn_prompt",
    )(lb2, z, z, z)


def _hgrn_sample_kernel(lb_ref, q_ref, f_ref, i_ref, s_ref, o_ref, s_out_ref, *, rows):
    r_i = lax.broadcasted_iota(jnp.int32, (HEAD_W, HEAD_W), 0)
    c_i = lax.broadcasted_iota(jnp.int32, (HEAD_W, HEAD_W), 1)
    eye = r_i == c_i

    def to_col(x):
        return jnp.sum(jnp.where(eye, x, 0.0), axis=1, keepdims=True)

    for h in range(HEADS):
        sl = slice(h * HEAD_W, (h + 1) * HEAD_W)
        lb = _lower_bound(lb_ref, sl)
        f_all = lb + (1.0 - lb) * jax.nn.sigmoid(f_ref[:, sl])
        q_all = q_ref[:, sl]
        i_all = i_ref[:, sl]
        o_rows = []
        for r in range(rows):
            f_col = to_col(f_all[r:r + 1])
            q_col = to_col(q_all[r:r + 1])
            s_new = f_col * s_ref[r, h] + (1.0 - f_col) * i_all[r:r + 1]
            s_out_ref[r, h] = s_new
            o_rows.append(jnp.sum(q_col * s_new, axis=0, keepdims=True))
        o_ref[:, sl] = jnp.concatenate(o_rows, axis=0)


def _hgrn_sample(lb2, z, state, rows):
    nb = state.shape[0]
    width = HEADS * HEAD_W
    col = lambda off: pl.BlockSpec((rows, width), lambda i: (i, off))
    sspec = pl.BlockSpec((rows, HEADS, HEAD_W, HEAD_W), lambda i: (i, 0, 0, 0))
    return pl.pallas_call(
        functools.partial(_hgrn_sample_kernel, rows=rows),
        grid=(nb // rows,),
        in_specs=[pl.BlockSpec(lb2.shape, lambda i: (0, 0)), col(3), col(4), col(5), sspec],
        out_specs=[pl.BlockSpec((rows, width), lambda i: (i, 0)), sspec],
        out_shape=[jax.ShapeDtypeStruct((nb, width), F32),
                   jax.ShapeDtypeStruct(state.shape, F32)],
        compiler_params=_cparams(("parallel",)),
        name="hgrn_sample",
    )(lb2, z, z, z, state)


def _head_rms(x, g):
    parts = []
    for h in range(HEADS):
        blk = x[:, h * HEAD_W:(h + 1) * HEAD_W]
        parts.append(_rms_rows(blk, g))
    return jnp.concatenate(parts, axis=1)


def _merge_kernel(oa_ref, ob_ref, og_ref, ga_ref, gb_ref, na_ref, nb_ref, wa_ref, wb_ref,
                  o_ref, a_ref, b_ref):
    @pl.when(pl.program_id(1) == 0)
    def _():
        a_ref[...] = (_head_rms(oa_ref[...], na_ref[...]) * (1.0 - LAM_INIT)).astype(BF16)
        og = og_ref[...].astype(F32)
        b_ref[...] = (_head_rms(ob_ref[...], nb_ref[...]) * (og * jax.nn.sigmoid(og))).astype(BF16)

    pa = jnp.dot(a_ref[...], wa_ref[...].astype(BF16), preferred_element_type=F32)
    pb = jnp.dot(b_ref[...], wb_ref[...].astype(BF16), preferred_element_type=F32)
    o_ref[...] = (jax.nn.sigmoid(ga_ref[...].astype(F32)) * pa
                  + jax.nn.sigmoid(gb_ref[...].astype(F32)) * pb).astype(o_ref.dtype)


def _merge(oa, ob, zg, na, nb, wa, wb, tm, tn):
    m, w = oa.shape
    n = wa.shape[1]
    goff = w // tn
    return pl.pallas_call(
        _merge_kernel,
        grid=(m // tm, n // tn),
        in_specs=[
            pl.BlockSpec((tm, w), lambda i, j: (i, 0)),
            pl.BlockSpec((tm, w), lambda i, j: (i, 0)),
            pl.BlockSpec((tm, w), lambda i, j: (i, 0)),
            pl.BlockSpec((tm, tn), lambda i, j: (i, goff + j)),
            pl.BlockSpec((tm, tn), lambda i, j: (i, goff + n // tn + j)),
            pl.BlockSpec((1, HEAD_W), lambda i, j: (0, 0)),
            pl.BlockSpec((1, HEAD_W), lambda i, j: (0, 0)),
            pl.BlockSpec((w, tn), lambda i, j: (0, j)),
            pl.BlockSpec((w, tn), lambda i, j: (0, j)),
        ],
        out_specs=pl.BlockSpec((tm, tn), lambda i, j: (i, j)),
        out_shape=jax.ShapeDtypeStruct((m, n), BF16),
        scratch_shapes=[pltpu.VMEM((tm, w), BF16), pltpu.VMEM((tm, w), BF16)],
        compiler_params=_cparams(("parallel", "arbitrary")),
        name="merge",
    )(oa, ob, zg, zg, zg, na, nb, wa, wb)


def _resmm_kernel(x_ref, a_ref, w_ref, o_ref):
    o_ref[...] = x_ref[...] + jnp.dot(a_ref[...], w_ref[...].astype(BF16), preferred_element_type=F32)


def _resmm(x, a, w, tm, tn):
    m, kdim = a.shape
    n = w.shape[1]
    return pl.pallas_call(
        _resmm_kernel,
        grid=(m // tm, n // tn),
        in_specs=[
            pl.BlockSpec((tm, tn), lambda i, j: (i, j)),
            pl.BlockSpec((tm, kdim), lambda i, j: (i, 0)),
            pl.BlockSpec((kdim, tn), lambda i, j: (0, j)),
        ],
        out_specs=pl.BlockSpec((tm, tn), lambda i, j: (i, j)),
        out_shape=jax.ShapeDtypeStruct((m, n), F32),
        compiler_params=_cparams(("parallel", "parallel")),
        name="resmm",
    )(x, a, w)


def _rope_tables(pos):
    half = HALF // 2
    inv = ROPE_THETA ** (-jnp.arange(half, dtype=F32) / half)
    ang = pos.astype(F32)[:, None] * inv[None, :]
    cos = jnp.tile(jnp.cos(ang), (1, LANES // half))
    sin = jnp.tile(jnp.sin(ang), (1, LANES // half))
    return cos, sin


def kernel(x_prompt, x_sample, cache_k, cache_v, state_hgrn, page_table, ffn1_norm, ffn1_w_gate, ffn1_w_up, ffn1_w_down, mix_norm, w_in, q_norm, k_norm, lambda_q1, lambda_k1, lambda_q2, lambda_k2, attn_sub_norm, hgrn_lower_bounds, hgrn_out_norm, w_proj_a, w_proj_b, w_out, ffn2_norm, ffn2_w_gate, ffn2_w_up, ffn2_w_down):
    batch, seq, d = x_prompt.shape
    nb = x_sample.shape[0]
    n_pages = page_table.shape[1]
    width = HEADS * HEAD_W
    n_a = 6 * width

    w1g, w1u, w1d = ffn1_w_gate[0], ffn1_w_up[0], ffn1_w_down[0]
    w2g, w2u, w2d = ffn2_w_gate[0], ffn2_w_up[0], ffn2_w_down[0]
    w_in_b = w_in[0]
    n_g = w_in_b.shape[1] - n_a
    wa, wb, wo = w_proj_a[0], w_proj_b[0], w_out[0]
    lam = jnp.concatenate([lambda_q1, lambda_k1, lambda_q2, lambda_k2], axis=0).astype(F32)
    gq = jnp.tile(q_norm.astype(F32), (1, 2))
    gk = jnp.tile(k_norm.astype(F32), (1, 2))
    lb2 = hgrn_lower_bounds.astype(F32)

    def front(x, tm_ffn, tm_in):
        x1 = _ffn(x, ffn1_norm, w1g, w1u, w1d, tm_ffn, 256)
        za = _normmm(x1, mix_norm, w_in_b, 0, n_a, F32, tm_in, 512)
        zg = _normmm(x1, mix_norm, w_in_b, n_a // 512, n_g, BF16, tm_in, 512)
        return x1, za, zg

    def back(x1, oa, ob, zg, tm, tm_ffn):
        mg = _merge(oa, ob, zg, attn_sub_norm, hgrn_out_norm, wa, wb, tm, 512)
        x2 = _resmm(x1, mg, wo, tm, 512)
        return _ffn(x2, ffn2_norm, w2g, w2u, w2d, tm_ffn, 256)

    xp = x_prompt.reshape(batch * seq, d)
    x1p, zap, zgp = front(xp, 1024, 1024)
    cos_p, sin_p = _rope_tables(jnp.arange(seq))
    qb, kb, vb, kf, vf = _prep_prompt(zap, cos_p, sin_p, gq, gk, batch, seq, 256)
    oa_p = _pattn(lam, qb, kb, vb, 512)
    ob_p, st_p = _hgrn_prompt(lb2, zap, batch, seq, 128)
    y_p = back(x1p, oa_p, ob_p, zgp, 512, 1024)

    xs = x_sample.reshape(nb, d)
    x1s, zas, zgs = front(xs, nb, nb)
    pos_s = jnp.full((nb,), n_pages * PAGE, jnp.int32)
    cos_s, sin_s = _rope_tables(pos_s)
    qs, ks, vs = _prep_sample(zas, cos_s, sin_s, gq, gk)
    ck = cache_k.reshape(cache_k.shape[1], PAGE * HEADS, HEAD_W)
    cv = cache_v.reshape(cache_v.shape[1], PAGE * HEADS, HEAD_W)
    tiles = lambda a: a.reshape(nb, HEADS, HEAD_W)
    oa_s = _sattn(page_table, lam, tiles(qs), tiles(ks), tiles(vs), ck, cv).reshape(nb, width)
    ob_s, st_s = _hgrn_sample(lb2, zas, state_hgrn[0], 8)
    y_s = back(x1s, oa_s, ob_s, zgs, nb, nb)

    return (y_p.reshape(batch, seq, d),
            y_s.reshape(nb, 1, d),
            kf.reshape(1, batch, seq, HEADS, HEAD_W),
            vf.reshape(1, batch, seq, HEADS, HEAD_W),
            ks.reshape(1, nb, 1, HEADS, HEAD_W),
            vs.reshape(1, nb, 1, HEADS, HEAD_W),
            st_p.reshape(1, batch, HEADS, HEAD_W, HEAD_W),
            st_s.reshape(1, nb, HEADS, HEAD_W, HEAD_W))
```

```python
import functools
import math

import jax
import jax.numpy as jnp
from jax import lax
from jax.experimental import pallas as pl
from jax.experimental.pallas import tpu as pltpu

F32 = jnp.float32
BF16 = jnp.bfloat16

EPS = 1e-6
ROPE_THETA = 10000.0
HEADS = 8
HEAD_W = 128
HALF = 64
PAGE = 128
LAM_INIT = 0.8 - 0.6 * math.exp(-0.3 * 0)
LANES = 128
VMEM_LIMIT = 56 * 1024 * 1024
NEG = -1e30


def _cparams(sem):
    return pltpu.CompilerParams(dimension_semantics=sem, vmem_limit_bytes=VMEM_LIMIT)


def _nt_dot(a, b):
    return lax.dot_general(a, b, (((1,), (1,)), ((), ())), preferred_element_type=F32)


def _rms_rows(x, g):
    ms = jnp.mean(x * x, axis=-1, keepdims=True)
    return x * lax.rsqrt(ms + EPS) * g


def _ffn_kernel(x_ref, g_ref, wg_ref, wu_ref, wd_ref, o_ref, h_ref):
    j = pl.program_id(1)

    @pl.when(j == 0)
    def _():
        x = x_ref[...]
        h_ref[...] = _rms_rows(x, g_ref[...]).astype(BF16)
        o_ref[...] = x

    h = h_ref[...]
    a = jnp.dot(h, wg_ref[...].astype(BF16), preferred_element_type=F32)
    u = jnp.dot(h, wu_ref[...].astype(BF16), preferred_element_type=F32)
    t = (a * jax.nn.sigmoid(a) * (0.5 * u)).astype(BF16)
    o_ref[...] += jnp.dot(t, wd_ref[...].astype(BF16), preferred_element_type=F32)


def _ffn(x, g, wg, wu, wd, tm, tf):
    m, d = x.shape
    f = wg.shape[1]
    return pl.pallas_call(
        _ffn_kernel,
        grid=(m // tm, f // tf),
        in_specs=[
            pl.BlockSpec((tm, d), lambda i, j: (i, 0)),
            pl.BlockSpec((1, d), lambda i, j: (0, 0)),
            pl.BlockSpec((d, tf), lambda i, j: (0, j)),
            pl.BlockSpec((d, tf), lambda i, j: (0, j)),
            pl.BlockSpec((tf, d), lambda i, j: (j, 0)),
        ],
        out_specs=pl.BlockSpec((tm, d), lambda i, j: (i, 0), pipeline_mode=pl.Buffered(1)),
        out_shape=jax.ShapeDtypeStruct((m, d), F32),
        scratch_shapes=[pltpu.VMEM((tm, d), BF16)],
        compiler_params=_cparams(("parallel", "arbitrary")),
        name="ffn",
    )(x, g, wg, wu, wd)


def _normmm_kernel(x_ref, g_ref, w_ref, o_ref, h_ref):
    @pl.when(pl.program_id(1) == 0)
    def _():
        h_ref[...] = _rms_rows(x_ref[...], g_ref[...]).astype(BF16)

    o_ref[...] = jnp.dot(h_ref[...], w_ref[...].astype(BF16), preferred_element_type=F32).astype(o_ref.dtype)


def _normmm(x, g, w, col_off, n, out_dtype, tm, tn):
    m, d = x.shape
    return pl.pallas_call(
        _normmm_kernel,
        grid=(m // tm, n // tn),
        in_specs=[
            pl.BlockSpec((tm, d), lambda i, j: (i, 0), pipeline_mode=pl.Buffered(1)),
            pl.BlockSpec((1, d), lambda i, j: (0, 0)),
            pl.BlockSpec((d, tn), lambda i, j: (0, col_off + j)),
        ],
        out_specs=pl.BlockSpec((tm, tn), lambda i, j: (i, j)),
        out_shape=jax.ShapeDtypeStruct((m, n), out_dtype),
        scratch_shapes=[pltpu.VMEM((tm, d), BF16)],
        compiler_params=_cparams(("parallel", "arbitrary")),
        name="normmm",
    )(x, g, w)


def _group_ms(x):
    r = lax.broadcasted_iota(jnp.int32, (LANES, LANES), 0)
    c = lax.broadcasted_iota(jnp.int32, (LANES, LANES), 1)
    grp = jnp.where((r >> 6) == (c >> 6), 1.0, 0.0).astype(BF16)
    x2 = x * x
    hi = x2.astype(BF16)
    r1 = x2 - hi.astype(F32)
    mid = r1.astype(BF16)
    lo = (r1 - mid.astype(F32)).astype(BF16)
    s = (jnp.dot(hi, grp, preferred_element_type=F32)
         + jnp.dot(mid, grp, preferred_element_type=F32)
         + jnp.dot(lo, grp, preferred_element_type=F32))
    return s * (1.0 / HALF)


def _norm_rope(x, g, cos, sin):
    y = x * lax.rsqrt(_group_ms(x) + EPS) * g
    lane = lax.broadcasted_iota(jnp.int32, y.shape, 1)
    first = (lane & (HALF - 1)) < (HALF // 2)
    rot = jnp.where(first, -pltpu.roll(y, LANES - HALF // 2, 1), pltpu.roll(y, HALF // 2, 1))
    return y * cos + rot * sin


def _prep_prompt_kernel(q_ref, k_ref, v_ref, cos_ref, sin_ref, gq_ref, gk_ref,
                        qb_ref, kb_ref, vb_ref, kf_ref, vf_ref):
    cos = cos_ref[...]
    sin = sin_ref[...]
    gq = gq_ref[...]
    gk = gk_ref[...]
    vf_ref[...] = v_ref[...]
    for h in range(HEADS):
        sl = slice(h * HEAD_W, (h + 1) * HEAD_W)
        k = _norm_rope(k_ref[:, sl], gk, cos, sin)
        kf_ref[:, sl] = k
        kb_ref[h] = k.astype(BF16)
        qb_ref[h] = (_norm_rope(q_ref[:, sl], gq, cos, sin) * (HALF ** -0.5)).astype(BF16)
        vb_ref[h] = v_ref[:, sl].astype(BF16)


def _prep_prompt(z, cos, sin, gq, gk, batch, seq, tm):
    nt = seq // tm
    width = HEADS * HEAD_W
    row = lambda b, i: b * nt + i
    hm = pl.BlockSpec((None, HEADS, tm, HEAD_W), lambda b, i: (b, 0, i, 0))
    flat = pl.BlockSpec((tm, width), lambda b, i: (row(b, i), 0))
    vec = pl.BlockSpec((1, HEAD_W), lambda b, i: (0, 0))
    tab = pl.BlockSpec((tm, HEAD_W), lambda b, i: (i, 0))
    hm_shape = jax.ShapeDtypeStruct((batch, HEADS, seq, HEAD_W), BF16)
    flat_shape = jax.ShapeDtypeStruct((batch * seq, width), F32)
    return pl.pallas_call(
        _prep_prompt_kernel,
        grid=(batch, nt),
        in_specs=[
            pl.BlockSpec((tm, width), lambda b, i: (row(b, i), 0)),
            pl.BlockSpec((tm, width), lambda b, i: (row(b, i), 1)),
            pl.BlockSpec((tm, width), lambda b, i: (row(b, i), 2)),
            tab, tab, vec, vec,
        ],
        out_specs=[hm, hm, hm, flat, flat],
        out_shape=[hm_shape, hm_shape, hm_shape, flat_shape, flat_shape],
        compiler_params=_cparams(("parallel", "parallel")),
        name="prep_prompt",
    )(z, z, z, cos, sin, gq, gk)


def _prep_sample_kernel(q_ref, k_ref, v_ref, cos_ref, sin_ref, gq_ref, gk_ref,
                        qf_ref, kf_ref, vf_ref):
    cos = cos_ref[...]
    sin = sin_ref[...]
    qf_ref[...] = _norm_rope(q_ref[...], gq_ref[...], cos, sin) * (HALF ** -0.5)
    kf_ref[...] = _norm_rope(k_ref[...], gk_ref[...], cos, sin)
    vf_ref[...] = v_ref[...]


def _prep_sample(z, cos, sin, gq, gk):
    m = z.shape[0]
    blk = lambda off: pl.BlockSpec((m, HEAD_W), lambda h: (0, off + h))
    vec = pl.BlockSpec((1, HEAD_W), lambda h: (0, 0))
    tab = pl.BlockSpec((m, HEAD_W), lambda h: (0, 0))
    shape = jax.ShapeDtypeStruct((m, HEADS * HEAD_W), F32)
    return pl.pallas_call(
        _prep_sample_kernel,
        grid=(HEADS,),
        in_specs=[blk(0), blk(HEADS), blk(2 * HEADS), tab, tab, vec, vec],
        out_specs=[blk(0), blk(0), blk(0)],
        out_shape=[shape, shape, shape],
        compiler_params=_cparams(("parallel",)),
        name="prep_sample",
    )(z, z, z, cos, sin, gq, gk)


def _lambda(lam_ref):
    l = lam_ref[...]
    s1 = jnp.sum(l[0:1] * l[1:2], axis=1, keepdims=True)
    s2 = jnp.sum(l[2:3] * l[3:4], axis=1, keepdims=True)
    return jnp.exp(s1) - jnp.exp(s2) + LAM_INIT


def _prompt_block(qi, lam, q_ref, k_ref, v_ref, o_ref, m_ref, l_ref, acc_ref, tq):
    q = q_ref[...]
    lane = lax.broadcasted_iota(jnp.int32, q.shape, 1)
    zero = jnp.zeros_like(q)
    qs = jnp.concatenate([jnp.where(lane < HALF, q, zero),
                          jnp.where(lane >= HALF, q, zero)], axis=0)

    m_ref[...] = jnp.full(m_ref.shape, NEG, F32)
    l_ref[...] = jnp.zeros(l_ref.shape, F32)
    acc_ref[...] = jnp.zeros(acc_ref.shape, F32)

    def step(kb, masked):
        start = pl.multiple_of(kb * tq, tq)
        k = k_ref[pl.ds(start, tq), :]
        v = v_ref[pl.ds(start, tq), :]
        s = _nt_dot(qs, k)
        if masked:
            r = lax.broadcasted_iota(jnp.int32, s.shape, 0)
            c = lax.broadcasted_iota(jnp.int32, s.shape, 1)
            r = jnp.where(r >= tq, r - tq, r)
            s = jnp.where(r >= c, s, NEG)
        m_prev = m_ref[...]
        m_new = jnp.maximum(m_prev, jnp.max(s, axis=1, keepdims=True))
        alpha = jnp.exp(m_prev - m_new)
        p = jnp.exp(s - jnp.concatenate([m_new] * (tq // LANES), axis=1))
        lsum = p[:, 0:LANES]
        for cblk in range(1, tq // LANES):
            lsum = lsum + p[:, cblk * LANES:(cblk + 1) * LANES]
        l_ref[...] = alpha * l_ref[...] + lsum
        acc_ref[...] = alpha * acc_ref[...] + jnp.dot(p.astype(BF16), v, preferred_element_type=F32)
        m_ref[...] = m_new

    def body(kb, carry):
        step(kb, False)
        return carry

    lax.fori_loop(0, qi, body, 0)
    step(qi, True)

    l = jnp.sum(l_ref[...], axis=1, keepdims=True)
    o = acc_ref[...] / l
    o_ref[...] = o[:tq] - lam * o[tq:]


def _sample_seq(lam, q, kn, vn, kpast, vpast):
    q2 = jnp.concatenate([q, q], axis=0)
    row = lax.broadcasted_iota(jnp.int32, q2.shape, 0)
    lane = lax.broadcasted_iota(jnp.int32, q2.shape, 1)
    qm = jnp.where((lane >> 6) == (row >> 3), q2, 0.0)

    s = _nt_dot(qm.astype(BF16), kpast.astype(BF16))
    own = ((lax.broadcasted_iota(jnp.int32, s.shape, 1) & (HEADS - 1))
           == (lax.broadcasted_iota(jnp.int32, s.shape, 0) & (HEADS - 1)))
    s = jnp.where(own, s, NEG)
    s_new = jnp.sum(qm * jnp.concatenate([kn, kn], axis=0), axis=1, keepdims=True)
    m = jnp.maximum(jnp.max(s, axis=1, keepdims=True), s_new)
    p = jnp.exp(s - m)
    p_new = jnp.exp(s_new - m)
    inv = 1.0 / (jnp.sum(p, axis=1, keepdims=True) + p_new)
    w = p * inv
    w_new = p_new * inv
    pd = w[:HEADS] - lam * w[HEADS:]
    pd_new = w_new[:HEADS] - lam * w_new[HEADS:]
    return jnp.dot(pd.astype(BF16), vpast.astype(BF16), preferred_element_type=F32) + pd_new * vn


def _attn_kernel(pt_ref, lam_ref, q_ref, k_ref, v_ref, qs_ref, kn_ref, vn_ref, ck_hbm, cv_hbm,
                 o_ref, os_ref, m_ref, l_ref, acc_ref, kbuf, vbuf, sem, *, tq, n_pages):
    qi = pl.program_id(2)
    seq = (pl.program_id(0) * pl.num_programs(1) + pl.program_id(1)) * pl.num_programs(2) + qi
    n_seq = pl.num_programs(0) * pl.num_programs(1) * pl.num_programs(2)
    slot = seq % 2
    page_rows = PAGE * HEADS

    def page_copies(seq_, slot_):
        cps = []
        for p in range(n_pages):
            pg = pt_ref[seq_, p]
            dst = pl.ds(p * page_rows, page_rows)
            cps.append(pltpu.make_async_copy(ck_hbm.at[pg], kbuf.at[slot_, dst, :], sem.at[0, slot_]))
            cps.append(pltpu.make_async_copy(cv_hbm.at[pg], vbuf.at[slot_, dst, :], sem.at[1, slot_]))
        return cps

    @pl.when(seq == 0)
    def _():
        for cp in page_copies(0, 0):
            cp.start()

    @pl.when(seq + 1 < n_seq)
    def _():
        for cp in page_copies(seq + 1, 1 - slot):
            cp.start()

    lam = _lambda(lam_ref)
    _prompt_block(qi, lam, q_ref, k_ref, v_ref, o_ref, m_ref, l_ref, acc_ref, tq)

    for cp in page_copies(seq, slot):
        cp.wait()
    os_ref[...] = _sample_seq(lam, qs_ref[...], kn_ref[...], vn_ref[...], kbuf[slot], vbuf[slot])


def _attn(page_table, lam, qb, kb, vb, qs, kn, vn, ck, cv, tq):
    batch, heads, seq, w = qb.shape
    nq = seq // tq
    nb, n_pages = page_table.shape
    assert batch * heads * nq == nb, "one sample sequence per prompt query block"
    rows = n_pages * PAGE * HEADS
    sidx = lambda b, h, i, pt: ((b * heads + h) * nq + i, 0, 0)
    tile = pl.BlockSpec((None, HEADS, HEAD_W), sidx)
    kv = pl.BlockSpec((None, None, seq, w), lambda b, h, i, pt: (b, h, 0, 0))
    grid_spec = pltpu.PrefetchScalarGridSpec(
        num_scalar_prefetch=1,
        grid=(batch, heads, nq),
        in_specs=[
            pl.BlockSpec(lam.shape, lambda b, h, i, pt: (0, 0)),
            pl.BlockSpec((None, None, tq, w), lambda b, h, i, pt: (b, h, i, 0)),
            kv, kv,
            tile, tile, tile,
            pl.BlockSpec(memory_space=pl.ANY),
            pl.BlockSpec(memory_space=pl.ANY),
        ],
        out_specs=[pl.BlockSpec((tq, w), lambda b, h, i, pt: (b * nq + i, h)), tile],
        scratch_shapes=[pltpu.VMEM((2 * tq, LANES), F32),
                        pltpu.VMEM((2 * tq, LANES), F32),
                        pltpu.VMEM((2 * tq, w), F32),
                        pltpu.VMEM((2, rows, HEAD_W), F32),
                        pltpu.VMEM((2, rows, HEAD_W), F32),
                        pltpu.SemaphoreType.DMA((2, 2))],
    )
    return pl.pallas_call(
        functools.partial(_attn_kernel, tq=tq, n_pages=n_pages),
        grid_spec=grid_spec,
        out_shape=[jax.ShapeDtypeStruct((batch * seq, heads * w), F32),
                   jax.ShapeDtypeStruct((nb, HEADS, HEAD_W), F32)],
        compiler_params=_cparams(("arbitrary", "arbitrary", "arbitrary")),
        name="attn",
    )(page_table, lam, qb, kb, vb, qs, kn, vn, ck, cv)


def _lower_bound(lb_ref, sl):
    a = lb_ref[:, sl]
    mx = jnp.maximum(a[0:1], a[1:2])
    e0 = jnp.exp(a[0:1] - mx)
    e1 = jnp.exp(a[1:2] - mx)
    return e0 / (e0 + e1)


def _split3(x):
    hi = x.astype(BF16)
    r1 = x - hi.astype(F32)
    mid = r1.astype(BF16)
    lo = (r1 - mid.astype(F32)).astype(BF16)
    return hi, mid, lo


def _pair_ref(b, m, c):
    parts = [jnp.broadcast_to(b[2 * m * p + m - 1:2 * m * p + m, :], (2 * m, b.shape[1]))
             for p in range(c // (2 * m))]
    return parts[0] if len(parts) == 1 else jnp.concatenate(parts, axis=0)


def _hgrn_prompt_kernel(lb_ref, q_ref, f_ref, i_ref, o_ref, s_out_ref, st_ref, *, c):
    ci = pl.program_id(1)

    @pl.when(ci == 0)
    def _():
        st_ref[...] = jnp.zeros(st_ref.shape, F32)

    row = lax.broadcasted_iota(jnp.int32, (c, c), 0)
    col = lax.broadcasted_iota(jnp.int32, (c, c), 1)
    tri = jnp.where(col <= row, 1.0, 0.0).astype(BF16)
    same_sub = (row >> 4) == (col >> 4)
    causal = col <= row
    mask16 = (row >> 5) == (col >> 5)
    mask32 = (row >> 6) == (col >> 6)
    trow = lax.broadcasted_iota(jnp.int32, (c, HEAD_W), 0)
    sub = 16

    for h in range(HEADS):
        sl = slice(h * HEAD_W, (h + 1) * HEAD_W)
        lb = _lower_bound(lb_ref, sl)
        q = q_ref[:, sl]
        v = i_ref[:, sl]
        f = lb + (1.0 - lb) * jax.nn.sigmoid(f_ref[:, sl])
        k = 1.0 - f
        hi, mid, lo = _split3(jnp.log(f))
        b = (jnp.dot(tri, hi, preferred_element_type=F32)
             + jnp.dot(tri, mid, preferred_element_type=F32)
             + jnp.dot(tri, lo, preferred_element_type=F32))

        ref_d = jnp.concatenate(
            [jnp.zeros((sub, HEAD_W), F32)]
            + [jnp.broadcast_to(b[sub * j - 1:sub * j, :], (sub, HEAD_W)) for j in range(1, c // sub)],
            axis=0)
        arg_d = b - ref_d
        p_d = _nt_dot((q * jnp.exp(arg_d)).astype(BF16), (k * jnp.exp(-arg_d)).astype(BF16))
        att = jnp.where(same_sub, jnp.where(causal, p_d, 0.0), 0.0)
        for m, msk in ((16, mask16), (32, mask32), (64, None)):
            odd = ((trow >> int(math.log2(m))) & 1) == 1
            d = b - _pair_ref(b, m, c)
            e = jnp.exp(jnp.where(odd, d, -d))
            qh = jnp.where(odd, q * e, 0.0).astype(BF16)
            kh = jnp.where(odd, 0.0, k * e).astype(BF16)
            p_m = _nt_dot(qh, kh)
            att = att + (p_m if msk is None else jnp.where(msk, p_m, 0.0))

        b_last = b[c - 1:c, :]
        st = st_ref[h]
        o = (jnp.dot(att.astype(BF16), v.astype(BF16), preferred_element_type=F32)
             + _nt_dot((q * jnp.exp(b)).astype(BF16), st.astype(BF16)))
        o_ref[:, sl] = o
        kt = (k * jnp.exp(b_last - b)).astype(BF16)
        st_new = st * jnp.exp(b_last) + jnp.dot(v.T.astype(BF16), kt, preferred_element_type=F32)
        st_ref[h] = st_new

    @pl.when(ci == pl.num_programs(1) - 1)
    def _():
        for h in range(HEADS):
            s_out_ref[h] = st_ref[h].T


def _hgrn_prompt(lb2, z, batch, seq, c):
    nc = seq // c
    width = HEADS * HEAD_W
    col = lambda off: pl.BlockSpec((c, width), lambda b, i: (b * nc + i, off))
    return pl.pallas_call(
        functools.partial(_hgrn_prompt_kernel, c=c),
        grid=(batch, nc),
        in_specs=[pl.BlockSpec(lb2.shape, lambda b, i: (0, 0)), col(3), col(4), col(5)],
        out_specs=[pl.BlockSpec((c, width), lambda b, i: (b * nc + i, 0)),
                   pl.BlockSpec((None, HEADS, HEAD_W, HEAD_W), lambda b, i: (b, 0, 0, 0))],
        out_shape=[jax.ShapeDtypeStruct((batch * seq, width), F32),
                   jax.ShapeDtypeStruct((batch, HEADS, HEAD_W, HEAD_W), F32)],
        scratch_shapes=[pltpu.VMEM((HEADS, HEAD_W, HEAD_W), F32)],
        compiler_params=_cparams(("parallel", "arbitrary")),
        name="hgrn_prompt",
    )(lb2, z, z, z)


def _hgrn_sample_kernel(lb_ref, q_ref, f_ref, i_ref, s_ref, o_ref, s_out_ref, *, rows):
    r_i = lax.broadcasted_iota(jnp.int32, (HEAD_W, HEAD_W), 0)
    c_i = lax.broadcasted_iota(jnp.int32, (HEAD_W, HEAD_W), 1)
    eye = r_i == c_i

    def to_col(x):
        return jnp.sum(jnp.where(eye, x, 0.0), axis=1, keepdims=True)

    for h in range(HEADS):
        sl = slice(h * HEAD_W, (h + 1) * HEAD_W)
        lb = _lower_bound(lb_ref, sl)
        f_all = lb + (1.0 - lb) * jax.nn.sigmoid(f_ref[:, sl])
        q_all = q_ref[:, sl]
        i_all = i_ref[:, sl]
        o_rows = []
        for r in range(rows):
            f_col = to_col(f_all[r:r + 1])
            q_col = to_col(q_all[r:r + 1])
            s_new = f_col * s_ref[r, h] + (1.0 - f_col) * i_all[r:r + 1]
            s_out_ref[r, h] = s_new
            o_rows.append(jnp.sum(q_col * s_new, axis=0, keepdims=True))
        o_ref[:, sl] = jnp.concatenate(o_rows, axis=0)


def _hgrn_sample(lb2, z, state, rows):
    nb = state.shape[0]
    width = HEADS * HEAD_W
    col = lambda off: pl.BlockSpec((rows, width), lambda i: (i, off))
    sspec = pl.BlockSpec((rows, HEADS, HEAD_W, HEAD_W), lambda i: (i, 0, 0, 0))
    return pl.pallas_call(
        functools.partial(_hgrn_sample_kernel, rows=rows),
        grid=(nb // rows,),
        in_specs=[pl.BlockSpec(lb2.shape, lambda i: (0, 0)), col(3), col(4), col(5), sspec],
        out_specs=[pl.BlockSpec((rows, width), lambda i: (i, 0)), sspec],
        out_shape=[jax.ShapeDtypeStruct((nb, width), F32),
                   jax.ShapeDtypeStruct(state.shape, F32)],
        compiler_params=_cparams(("parallel",)),
        name="hgrn_sample",
    )(lb2, z, z, z, state)


def _head_rms(x, g):
    parts = []
    for h in range(HEADS):
        blk = x[:, h * HEAD_W:(h + 1) * HEAD_W]
        parts.append(_rms_rows(blk, g))
    return jnp.concatenate(parts, axis=1)


def _merge_kernel(oa_ref, ob_ref, og_ref, ga_ref, gb_ref, na_ref, nb_ref, wa_ref, wb_ref,
                  o_ref, a_ref, b_ref):
    @pl.when(pl.program_id(1) == 0)
    def _():
        a_ref[...] = (_head_rms(oa_ref[...], na_ref[...]) * (1.0 - LAM_INIT)).astype(BF16)
        og = og_ref[...].astype(F32)
        b_ref[...] = (_head_rms(ob_ref[...], nb_ref[...]) * (og * jax.nn.sigmoid(og))).astype(BF16)

    pa = jnp.dot(a_ref[...], wa_ref[...].astype(BF16), preferred_element_type=F32)
    pb = jnp.dot(b_ref[...], wb_ref[...].astype(BF16), preferred_element_type=F32)
    o_ref[...] = (jax.nn.sigmoid(ga_ref[...].astype(F32)) * pa
                  + jax.nn.sigmoid(gb_ref[...].astype(F32)) * pb).astype(o_ref.dtype)


def _merge(oa, ob, zg, na, nb, wa, wb, tm, tn):
    m, w = oa.shape
    n = wa.shape[1]
    goff = w // tn
    return pl.pallas_call(
        _merge_kernel,
        grid=(m // tm, n // tn),
        in_specs=[
            pl.BlockSpec((tm, w), lambda i, j: (i, 0)),
            pl.BlockSpec((tm, w), lambda i, j: (i, 0)),
            pl.BlockSpec((tm, w), lambda i, j: (i, 0)),
            pl.BlockSpec((tm, tn), lambda i, j: (i, goff + j)),
            pl.BlockSpec((tm, tn), lambda i, j: (i, goff + n // tn + j)),
            pl.BlockSpec((1, HEAD_W), lambda i, j: (0, 0)),
            pl.BlockSpec((1, HEAD_W), lambda i, j: (0, 0)),
            pl.BlockSpec((w, tn), lambda i, j: (0, j)),
            pl.BlockSpec((w, tn), lambda i, j: (0, j)),
        ],
        out_specs=pl.BlockSpec((tm, tn), lambda i, j: (i, j)),
        out_shape=jax.ShapeDtypeStruct((m, n), BF16),
        scratch_shapes=[pltpu.VMEM((tm, w), BF16), pltpu.VMEM((tm, w), BF16)],
        compiler_params=_cparams(("parallel", "arbitrary")),
        name="merge",
    )(oa, ob, zg, zg, zg, na, nb, wa, wb)


def _resmm_kernel(x_ref, a_ref, w_ref, o_ref):
    o_ref[...] = x_ref[...] + jnp.dot(a_ref[...], w_ref[...].astype(BF16), preferred_element_type=F32)


def _resmm(x, a, w, tm, tn):
    m, kdim = a.shape
    n = w.shape[1]
    return pl.pallas_call(
        _resmm_kernel,
        grid=(m // tm, n // tn),
        in_specs=[
            pl.BlockSpec((tm, tn), lambda i, j: (i, j)),
            pl.BlockSpec((tm, kdim), lambda i, j: (i, 0)),
            pl.BlockSpec((kdim, tn), lambda i, j: (0, j)),
        ],
        out_specs=pl.BlockSpec((tm, tn), lambda i, j: (i, j)),
        out_shape=jax.ShapeDtypeStruct((m, n), F32),
        compiler_params=_cparams(("parallel", "parallel")),
        name="resmm",
    )(x, a, w)


def _rope_tables(pos):
    half = HALF // 2
    inv = ROPE_THETA ** (-jnp.arange(half, dtype=F32) / half)
    ang = pos.astype(F32)[:, None] * inv[None, :]
    cos = jnp.tile(jnp.cos(ang), (1, LANES // half))
    sin = jnp.tile(jnp.sin(ang), (1, LANES // half))
    return cos, sin


def kernel(x_prompt, x_sample, cache_k, cache_v, state_hgrn, page_table, ffn1_norm, ffn1_w_gate, ffn1_w_up, ffn1_w_down, mix_norm, w_in, q_norm, k_norm, lambda_q1, lambda_k1, lambda_q2, lambda_k2, attn_sub_norm, hgrn_lower_bounds, hgrn_out_norm, w_proj_a, w_proj_b, w_out, ffn2_norm, ffn2_w_gate, ffn2_w_up, ffn2_w_down):
    batch, seq, d = x_prompt.shape
    nb = x_sample.shape[0]
    n_pages = page_table.shape[1]
    width = HEADS * HEAD_W
    n_a = 6 * width

    w1g, w1u, w1d = ffn1_w_gate[0], ffn1_w_up[0], ffn1_w_down[0]
    w2g, w2u, w2d = ffn2_w_gate[0], ffn2_w_up[0], ffn2_w_down[0]
    w_in_b = w_in[0]
    n_g = w_in_b.shape[1] - n_a
    wa, wb, wo = w_proj_a[0], w_proj_b[0], w_out[0]
    lam = jnp.concatenate([lambda_q1, lambda_k1, lambda_q2, lambda_k2], axis=0).astype(F32)
    gq = jnp.tile(q_norm.astype(F32), (1, 2))
    gk = jnp.tile(k_norm.astype(F32), (1, 2))
    lb2 = hgrn_lower_bounds.astype(F32)

    def front(x, tm_ffn, tm_in):
        x1 = _ffn(x, ffn1_norm, w1g, w1u, w1d, tm_ffn, 256)
        za = _normmm(x1, mix_norm, w_in_b, 0, n_a, F32, tm_in, 512)
        zg = _normmm(x1, mix_norm, w_in_b, n_a // 512, n_g, BF16, tm_in, 512)
        return x1, za, zg

    def back(x1, oa, ob, zg, tm, tm_ffn):
        mg = _merge(oa, ob, zg, attn_sub_norm, hgrn_out_norm, wa, wb, tm, 512)
        x2 = _resmm(x1, mg, wo, tm, 512)
        return _ffn(x2, ffn2_norm, w2g, w2u, w2d, tm_ffn, 256)

    xp = x_prompt.reshape(batch * seq, d)
    x1p, zap, zgp = front(xp, 1024, 2048)
    cos_p, sin_p = _rope_tables(jnp.arange(seq))
    qb, kb, vb, kf, vf = _prep_prompt(zap, cos_p, sin_p, gq, gk, batch, seq, 256)
    xs = x_sample.reshape(nb, d)
    x1s, zas, zgs = front(xs, nb, nb)
    pos_s = jnp.full((nb,), n_pages * PAGE, jnp.int32)
    cos_s, sin_s = _rope_tables(pos_s)
    qs, ks, vs = _prep_sample(zas, cos_s, sin_s, gq, gk)

    ck = cache_k.reshape(cache_k.shape[1], PAGE * HEADS, HEAD_W)
    cv = cache_v.reshape(cache_v.shape[1], PAGE * HEADS, HEAD_W)
    tiles = lambda a: a.reshape(nb, HEADS, HEAD_W)
    oa_p, oa_s = _attn(page_table, lam, qb, kb, vb, tiles(qs), tiles(ks), tiles(vs), ck, cv, 512)
    oa_s = oa_s.reshape(nb, width)

    ob_p, st_p = _hgrn_prompt(lb2, zap, batch, seq, 128)
    y_p = back(x1p, oa_p, ob_p, zgp, 1024, 1024)
    ob_s, st_s = _hgrn_sample(lb2, zas, state_hgrn[0], 8)
    y_s = back(x1s, oa_s, ob_s, zgs, nb, nb)

    return (y_p.reshape(batch, seq, d),
            y_s.reshape(nb, 1, d),
            kf.reshape(1, batch, seq, HEADS, HEAD_W),
            vf.reshape(1, batch, seq, HEADS, HEAD_W),
            ks.reshape(1, nb, 1, HEADS, HEAD_W),
            vs.reshape(1, nb, 1, HEADS, HEAD_W),
            st_p.reshape(1, batch, HEADS, HEAD_W, HEAD_W),
            st_s.reshape(1, nb, HEADS, HEAD_W, HEAD_W))
```

```python
import functools
import math

import jax
import jax.numpy as jnp
from jax import lax
from jax.experimental import pallas as pl
from jax.experimental.pallas import tpu as pltpu

F32 = jnp.float32
BF16 = jnp.bfloat16

EPS = 1e-6
ROPE_THETA = 10000.0
HEADS = 8
HEAD_W = 128
HALF = 64
PAGE = 128
LAM_INIT = 0.8 - 0.6 * math.exp(-0.3 * 0)
LANES = 128
VMEM_LIMIT = 56 * 1024 * 1024
NEG = -1e30


def _cparams(sem):
    return pltpu.CompilerParams(dimension_semantics=sem, vmem_limit_bytes=VMEM_LIMIT)


def _nt_dot(a, b):
    return lax.dot_general(a, b, (((1,), (1,)), ((), ())), preferred_element_type=F32)


def _rms_rows(x, g):
    ms = jnp.mean(x * x, axis=-1, keepdims=True)
    return x * lax.rsqrt(ms + EPS) * g


def _ffn_kernel(xp_ref, xs_ref, g_ref, wg_ref, wu_ref, wd_ref, op_ref, os_ref, h_ref):
    j = pl.program_id(1)
    tp = xp_ref.shape[0]

    @pl.when(j == 0)
    def _():
        xp = xp_ref[...]
        xs = xs_ref[...]
        g = g_ref[...]
        h_ref[:tp] = _rms_rows(xp, g).astype(BF16)
        h_ref[tp:] = _rms_rows(xs, g).astype(BF16)
        op_ref[...] = xp
        os_ref[...] = xs

    h = h_ref[...]
    a = jnp.dot(h, wg_ref[...].astype(BF16), preferred_element_type=F32)
    u = jnp.dot(h, wu_ref[...].astype(BF16), preferred_element_type=F32)
    t = (a * jax.nn.sigmoid(a) * (0.5 * u)).astype(BF16)
    y = jnp.dot(t, wd_ref[...].astype(BF16), preferred_element_type=F32)
    op_ref[...] += y[:tp]
    os_ref[...] += y[tp:]


def _ffn(xp, xs, g, wg, wu, wd, tm, tf):
    m, d = xp.shape
    f = wg.shape[1]
    nt = m // tm
    ts = xs.shape[0] // nt
    rows = lambda t: pl.BlockSpec((t, d), lambda i, j: (i, 0))
    acc = lambda t: pl.BlockSpec((t, d), lambda i, j: (i, 0), pipeline_mode=pl.Buffered(1))
    return pl.pallas_call(
        _ffn_kernel,
        grid=(nt, f // tf),
        in_specs=[
            rows(tm), rows(ts),
            pl.BlockSpec((1, d), lambda i, j: (0, 0)),
            pl.BlockSpec((d, tf), lambda i, j: (0, j)),
            pl.BlockSpec((d, tf), lambda i, j: (0, j)),
            pl.BlockSpec((tf, d), lambda i, j: (j, 0)),
        ],
        out_specs=[acc(tm), acc(ts)],
        out_shape=[jax.ShapeDtypeStruct(xp.shape, F32), jax.ShapeDtypeStruct(xs.shape, F32)],
        scratch_shapes=[pltpu.VMEM((tm + ts, d), BF16)],
        compiler_params=_cparams(("parallel", "arbitrary")),
        name="ffn",
    )(xp, xs, g, wg, wu, wd)


def _normmm_kernel(xp_ref, xs_ref, g_ref, w_ref, op_ref, os_ref, h_ref):
    tp = xp_ref.shape[0]

    @pl.when(pl.program_id(1) == 0)
    def _():
        g = g_ref[...]
        h_ref[:tp] = _rms_rows(xp_ref[...], g).astype(BF16)
        h_ref[tp:] = _rms_rows(xs_ref[...], g).astype(BF16)

    z = jnp.dot(h_ref[...], w_ref[...].astype(BF16), preferred_element_type=F32)
    op_ref[...] = z[:tp].astype(op_ref.dtype)
    os_ref[...] = z[tp:].astype(os_ref.dtype)


def _normmm(xp, xs, g, w, col_off, n, out_dtype, tm, tn):
    m, d = xp.shape
    nt = m // tm
    ts = xs.shape[0] // nt
    rows = lambda t: pl.BlockSpec((t, d), lambda i, j: (i, 0), pipeline_mode=pl.Buffered(1))
    return pl.pallas_call(
        _normmm_kernel,
        grid=(nt, n // tn),
        in_specs=[
            rows(tm), rows(ts),
            pl.BlockSpec((1, d), lambda i, j: (0, 0)),
            pl.BlockSpec((d, tn), lambda i, j: (0, col_off + j)),
        ],
        out_specs=[pl.BlockSpec((tm, tn), lambda i, j: (i, j)), pl.BlockSpec((ts, tn), lambda i, j: (i, j))],
        out_shape=[jax.ShapeDtypeStruct((m, n), out_dtype), jax.ShapeDtypeStruct((xs.shape[0], n), out_dtype)],
        scratch_shapes=[pltpu.VMEM((tm + ts, d), BF16)],
        compiler_params=_cparams(("parallel", "arbitrary")),
        name="normmm",
    )(xp, xs, g, w)


def _group_ms(x):
    r = lax.broadcasted_iota(jnp.int32, (LANES, LANES), 0)
    c = lax.broadcasted_iota(jnp.int32, (LANES, LANES), 1)
    grp = jnp.where((r >> 6) == (c >> 6), 1.0, 0.0).astype(BF16)
    x2 = x * x
    hi = x2.astype(BF16)
    r1 = x2 - hi.astype(F32)
    mid = r1.astype(BF16)
    lo = (r1 - mid.astype(F32)).astype(BF16)
    s = (jnp.dot(hi, grp, preferred_element_type=F32)
         + jnp.dot(mid, grp, preferred_element_type=F32)
         + jnp.dot(lo, grp, preferred_element_type=F32))
    return s * (1.0 / HALF)


def _norm_rope(x, g, cos, sin):
    y = x * lax.rsqrt(_group_ms(x) + EPS) * g
    lane = lax.broadcasted_iota(jnp.int32, y.shape, 1)
    first = (lane & (HALF - 1)) < (HALF // 2)
    rot = jnp.where(first, -pltpu.roll(y, LANES - HALF // 2, 1), pltpu.roll(y, HALF // 2, 1))
    return y * cos + rot * sin


def _prep_prompt_kernel(q_ref, k_ref, v_ref, cos_ref, sin_ref, gq_ref, gk_ref,
                        qb_ref, kb_ref, vb_ref, kf_ref, vf_ref):
    cos = cos_ref[...]
    sin = sin_ref[...]
    gq = gq_ref[...]
    gk = gk_ref[...]
    vf_ref[...] = v_ref[...]
    for h in range(HEADS):
        sl = slice(h * HEAD_W, (h + 1) * HEAD_W)
        k = _norm_rope(k_ref[:, sl], gk, cos, sin)
        kf_ref[:, sl] = k
        kb_ref[h] = k.astype(BF16)
        qb_ref[h] = (_norm_rope(q_ref[:, sl], gq, cos, sin) * (HALF ** -0.5)).astype(BF16)
        vb_ref[h] = v_ref[:, sl].astype(BF16)


def _prep_prompt(z, cos, sin, gq, gk, batch, seq, tm):
    nt = seq // tm
    width = HEADS * HEAD_W
    row = lambda b, i: b * nt + i
    hm = pl.BlockSpec((None, HEADS, tm, HEAD_W), lambda b, i: (b, 0, i, 0))
    flat = pl.BlockSpec((tm, width), lambda b, i: (row(b, i), 0))
    vec = pl.BlockSpec((1, HEAD_W), lambda b, i: (0, 0))
    tab = pl.BlockSpec((tm, HEAD_W), lambda b, i: (i, 0))
    hm_shape = jax.ShapeDtypeStruct((batch, HEADS, seq, HEAD_W), BF16)
    flat_shape = jax.ShapeDtypeStruct((batch * seq, width), F32)
    return pl.pallas_call(
        _prep_prompt_kernel,
        grid=(batch, nt),
        in_specs=[
            pl.BlockSpec((tm, width), lambda b, i: (row(b, i), 0)),
            pl.BlockSpec((tm, width), lambda b, i: (row(b, i), 1)),
            pl.BlockSpec((tm, width), lambda b, i: (row(b, i), 2)),
            tab, tab, vec, vec,
        ],
        out_specs=[hm, hm, hm, flat, flat],
        out_shape=[hm_shape, hm_shape, hm_shape, flat_shape, flat_shape],
        compiler_params=_cparams(("parallel", "parallel")),
        name="prep_prompt",
    )(z, z, z, cos, sin, gq, gk)


def _prep_sample_kernel(q_ref, k_ref, v_ref, cos_ref, sin_ref, gq_ref, gk_ref,
                        qf_ref, kf_ref, vf_ref):
    cos = cos_ref[...]
    sin = sin_ref[...]
    qf_ref[...] = _norm_rope(q_ref[...], gq_ref[...], cos, sin) * (HALF ** -0.5)
    kf_ref[...] = _norm_rope(k_ref[...], gk_ref[...], cos, sin)
    vf_ref[...] = v_ref[...]


def _prep_sample(z, cos, sin, gq, gk):
    m = z.shape[0]
    blk = lambda off: pl.BlockSpec((m, HEAD_W), lambda h: (0, off + h))
    vec = pl.BlockSpec((1, HEAD_W), lambda h: (0, 0))
    tab = pl.BlockSpec((m, HEAD_W), lambda h: (0, 0))
    shape = jax.ShapeDtypeStruct((m, HEADS * HEAD_W), F32)
    return pl.pallas_call(
        _prep_sample_kernel,
        grid=(HEADS,),
        in_specs=[blk(0), blk(HEADS), blk(2 * HEADS), tab, tab, vec, vec],
        out_specs=[blk(0), blk(0), blk(0)],
        out_shape=[shape, shape, shape],
        compiler_params=_cparams(("parallel",)),
        name="prep_sample",
    )(z, z, z, cos, sin, gq, gk)


def _lambda(lam_ref):
    l = lam_ref[...]
    s1 = jnp.sum(l[0:1] * l[1:2], axis=1, keepdims=True)
    s2 = jnp.sum(l[2:3] * l[3:4], axis=1, keepdims=True)
    return jnp.exp(s1) - jnp.exp(s2) + LAM_INIT


def _prompt_block(qi, lam, q_ref, k_ref, v_ref, o_ref, m_ref, l_ref, acc_ref, tq):
    q = q_ref[...]
    lane = lax.broadcasted_iota(jnp.int32, q.shape, 1)
    zero = jnp.zeros_like(q)
    qs = jnp.concatenate([jnp.where(lane < HALF, q, zero),
                          jnp.where(lane >= HALF, q, zero)], axis=0)

    m_ref[...] = jnp.full(m_ref.shape, NEG, F32)
    l_ref[...] = jnp.zeros(l_ref.shape, F32)
    acc_ref[...] = jnp.zeros(acc_ref.shape, F32)

    def step(kb, masked):
        start = pl.multiple_of(kb * tq, tq)
        k = k_ref[pl.ds(start, tq), :]
        v = v_ref[pl.ds(start, tq), :]
        s = _nt_dot(qs, k)
        if masked:
            r = lax.broadcasted_iota(jnp.int32, s.shape, 0)
            c = lax.broadcasted_iota(jnp.int32, s.shape, 1)
            r = jnp.where(r >= tq, r - tq, r)
            s = jnp.where(r >= c, s, NEG)
        m_prev = m_ref[...]
        m_new = jnp.maximum(m_prev, jnp.max(s, axis=1, keepdims=True))
        alpha = jnp.exp(m_prev - m_new)
        p = jnp.exp(s - jnp.concatenate([m_new] * (tq // LANES), axis=1))
        lsum = p[:, 0:LANES]
        for cblk in range(1, tq // LANES):
            lsum = lsum + p[:, cblk * LANES:(cblk + 1) * LANES]
        l_ref[...] = alpha * l_ref[...] + lsum
        acc_ref[...] = alpha * acc_ref[...] + jnp.dot(p.astype(BF16), v, preferred_element_type=F32)
        m_ref[...] = m_new

    def body(kb, carry):
        step(kb, False)
        return carry

    lax.fori_loop(0, qi, body, 0)
    step(qi, True)

    l = jnp.sum(l_ref[...], axis=1, keepdims=True)
    o = acc_ref[...] / l
    o_ref[...] = o[:tq] - lam * o[tq:]


def _sample_seq(lam, q, kn, vn, kpast, vpast):
    q2 = jnp.concatenate([q, q], axis=0)
    row = lax.broadcasted_iota(jnp.int32, q2.shape, 0)
    lane = lax.broadcasted_iota(jnp.int32, q2.shape, 1)
    qm = jnp.where((lane >> 6) == (row >> 3), q2, 0.0)

    s = _nt_dot(qm.astype(BF16), kpast.astype(BF16))
    own = ((lax.broadcasted_iota(jnp.int32, s.shape, 1) & (HEADS - 1))
           == (lax.broadcasted_iota(jnp.int32, s.shape, 0) & (HEADS - 1)))
    s = jnp.where(own, s, NEG)
    s_new = jnp.sum(qm * jnp.concatenate([kn, kn], axis=0), axis=1, keepdims=True)
    m = jnp.maximum(jnp.max(s, axis=1, keepdims=True), s_new)
    p = jnp.exp(s - m)
    p_new = jnp.exp(s_new - m)
    inv = 1.0 / (jnp.sum(p, axis=1, keepdims=True) + p_new)
    w = p * inv
    w_new = p_new * inv
    pd = w[:HEADS] - lam * w[HEADS:]
    pd_new = w_new[:HEADS] - lam * w_new[HEADS:]
    return jnp.dot(pd.astype(BF16), vpast.astype(BF16), preferred_element_type=F32) + pd_new * vn


def _attn_kernel(pt_ref, lam_ref, q_ref, k_ref, v_ref, qs_ref, kn_ref, vn_ref, ck_hbm, cv_hbm,
                 o_ref, os_ref, m_ref, l_ref, acc_ref, kbuf, vbuf, sem, *, tq, n_pages):
    qi = pl.program_id(2)
    seq = (pl.program_id(0) * pl.num_programs(1) + pl.program_id(1)) * pl.num_programs(2) + qi
    n_seq = pl.num_programs(0) * pl.num_programs(1) * pl.num_programs(2)
    slot = seq % 2
    page_rows = PAGE * HEADS

    def page_copies(seq_, slot_):
        cps = []
        for p in range(n_pages):
            pg = pt_ref[seq_, p]
            dst = pl.ds(p * page_rows, page_rows)
            cps.append(pltpu.make_async_copy(ck_hbm.at[pg], kbuf.at[slot_, dst, :], sem.at[0, slot_]))
            cps.append(pltpu.make_async_copy(cv_hbm.at[pg], vbuf.at[slot_, dst, :], sem.at[1, slot_]))
        return cps

    @pl.when(seq == 0)
    def _():
        for cp in page_copies(0, 0):
            cp.start()

    @pl.when(seq + 1 < n_seq)
    def _():
        for cp in page_copies(seq + 1, 1 - slot):
            cp.start()

    lam = _lambda(lam_ref)
    _prompt_block(qi, lam, q_ref, k_ref, v_ref, o_ref, m_ref, l_ref, acc_ref, tq)

    for cp in page_copies(seq, slot):
        cp.wait()
    os_ref[...] = _sample_seq(lam, qs_ref[...], kn_ref[...], vn_ref[...], kbuf[slot], vbuf[slot])


def _attn(page_table, lam, qb, kb, vb, qs, kn, vn, ck, cv, tq):
    batch, heads, seq, w = qb.shape
    nq = seq // tq
    nb, n_pages = page_table.shape
    assert batch * heads * nq == nb, "one sample sequence per prompt query block"
    rows = n_pages * PAGE * HEADS
    sidx = lambda b, h, i, pt: ((b * heads + h) * nq + i, 0, 0)
    tile = pl.BlockSpec((None, HEADS, HEAD_W), sidx)
    kv = pl.BlockSpec((None, None, seq, w), lambda b, h, i, pt: (b, h, 0, 0))
    grid_spec = pltpu.PrefetchScalarGridSpec(
        num_scalar_prefetch=1,
        grid=(batch, heads, nq),
        in_specs=[
            pl.BlockSpec(lam.shape, lambda b, h, i, pt: (0, 0)),
            pl.BlockSpec((None, None, tq, w), lambda b, h, i, pt: (b, h, i, 0)),
            kv, kv,
            tile, tile, tile,
            pl.BlockSpec(memory_space=pl.ANY),
            pl.BlockSpec(memory_space=pl.ANY),
        ],
        out_specs=[pl.BlockSpec((tq, w), lambda b, h, i, pt: (b * nq + i, h)), tile],
        scratch_shapes=[pltpu.VMEM((2 * tq, LANES), F32),
                        pltpu.VMEM((2 * tq, LANES), F32),
                        pltpu.VMEM((2 * tq, w), F32),
                        pltpu.VMEM((2, rows, HEAD_W), F32),
                        pltpu.VMEM((2, rows, HEAD_W), F32),
                        pltpu.SemaphoreType.DMA((2, 2))],
    )
    return pl.pallas_call(
        functools.partial(_attn_kernel, tq=tq, n_pages=n_pages),
        grid_spec=grid_spec,
        out_shape=[jax.ShapeDtypeStruct((batch * seq, heads * w), F32),
                   jax.ShapeDtypeStruct((nb, HEADS, HEAD_W), F32)],
        compiler_params=_cparams(("arbitrary", "arbitrary", "arbitrary")),
        name="attn",
    )(page_table, lam, qb, kb, vb, qs, kn, vn, ck, cv)


def _lower_bound(lb_ref, sl):
    a = lb_ref[:, sl]
    mx = jnp.maximum(a[0:1], a[1:2])
    e0 = jnp.exp(a[0:1] - mx)
    e1 = jnp.exp(a[1:2] - mx)
    return e0 / (e0 + e1)


def _split3(x):
    hi = x.astype(BF16)
    r1 = x - hi.astype(F32)
    mid = r1.astype(BF16)
    lo = (r1 - mid.astype(F32)).astype(BF16)
    return hi, mid, lo


def _pair_ref(b, m, c):
    parts = [jnp.broadcast_to(b[2 * m * p + m - 1:2 * m * p + m, :], (2 * m, b.shape[1]))
             for p in range(c // (2 * m))]
    return parts[0] if len(parts) == 1 else jnp.concatenate(parts, axis=0)


def _hgrn_prompt_kernel(lb_ref, q_ref, f_ref, i_ref, o_ref, s_out_ref, st_ref, *, c):
    ci = pl.program_id(1)

    @pl.when(ci == 0)
    def _():
        st_ref[...] = jnp.zeros(st_ref.shape, F32)

    row = lax.broadcasted_iota(jnp.int32, (c, c), 0)
    col = lax.broadcasted_iota(jnp.int32, (c, c), 1)
    tri = jnp.where(col <= row, 1.0, 0.0).astype(BF16)
    same_sub = (row >> 4) == (col >> 4)
    causal = col <= row
    mask16 = (row >> 5) == (col >> 5)
    mask32 = (row >> 6) == (col >> 6)
    trow = lax.broadcasted_iota(jnp.int32, (c, HEAD_W), 0)
    sub = 16

    for h in range(HEADS):
        sl = slice(h * HEAD_W, (h + 1) * HEAD_W)
        lb = _lower_bound(lb_ref, sl)
        q = q_ref[:, sl]
        v = i_ref[:, sl]
        f = lb + (1.0 - lb) * jax.nn.sigmoid(f_ref[:, sl])
        k = 1.0 - f
        hi, mid, lo = _split3(jnp.log(f))
        b = (jnp.dot(tri, hi, preferred_element_type=F32)
             + jnp.dot(tri, mid, preferred_element_type=F32)
             + jnp.dot(tri, lo, preferred_element_type=F32))

        ref_d = jnp.concatenate(
            [jnp.zeros((sub, HEAD_W), F32)]
            + [jnp.broadcast_to(b[sub * j - 1:sub * j, :], (sub, HEAD_W)) for j in range(1, c // sub)],
            axis=0)
        arg_d = b - ref_d
        p_d = _nt_dot((q * jnp.exp(arg_d)).astype(BF16), (k * jnp.exp(-arg_d)).astype(BF16))
        att = jnp.where(same_sub, jnp.where(causal, p_d, 0.0), 0.0)
        for m, msk in ((16, mask16), (32, mask32), (64, None)):
            odd = ((trow >> int(math.log2(m))) & 1) == 1
            d = b - _pair_ref(b, m, c)
            e = jnp.exp(jnp.where(odd, d, -d))
            qh = jnp.where(odd, q * e, 0.0).astype(BF16)
            kh = jnp.where(odd, 0.0, k * e).astype(BF16)
            p_m = _nt_dot(qh, kh)
            att = att + (p_m if msk is None else jnp.where(msk, p_m, 0.0))

        b_last = b[c - 1:c, :]
        st = st_ref[h]
        o = (jnp.dot(att.astype(BF16), v.astype(BF16), preferred_element_type=F32)
             + _nt_dot((q * jnp.exp(b)).astype(BF16), st.astype(BF16)))
        o_ref[:, sl] = o
        kt = (k * jnp.exp(b_last - b)).astype(BF16)
        st_new = st * jnp.exp(b_last) + jnp.dot(v.T.astype(BF16), kt, preferred_element_type=F32)
        st_ref[h] = st_new

    @pl.when(ci == pl.num_programs(1) - 1)
    def _():
        for h in range(HEADS):
            s_out_ref[h] = st_ref[h].T


def _hgrn_prompt(lb2, z, batch, seq, c):
    nc = seq // c
    width = HEADS * HEAD_W
    col = lambda off: pl.BlockSpec((c, width), lambda b, i: (b * nc + i, off))
    return pl.pallas_call(
        functools.partial(_hgrn_prompt_kernel, c=c),
        grid=(batch, nc),
        in_specs=[pl.BlockSpec(lb2.shape, lambda b, i: (0, 0)), col(3), col(4), col(5)],
        out_specs=[pl.BlockSpec((c, width), lambda b, i: (b * nc + i, 0)),
                   pl.BlockSpec((None, HEADS, HEAD_W, HEAD_W), lambda b, i: (b, 0, 0, 0))],
        out_shape=[jax.ShapeDtypeStruct((batch * seq, width), F32),
                   jax.ShapeDtypeStruct((batch, HEADS, HEAD_W, HEAD_W), F32)],
        scratch_shapes=[pltpu.VMEM((HEADS, HEAD_W, HEAD_W), F32)],
        compiler_params=_cparams(("parallel", "arbitrary")),
        name="hgrn_prompt",
    )(lb2, z, z, z)


def _hgrn_sample_kernel(lb_ref, q_ref, f_ref, i_ref, s_ref, o_ref, s_out_ref, *, rows):
    r_i = lax.broadcasted_iota(jnp.int32, (HEAD_W, HEAD_W), 0)
    c_i = lax.broadcasted_iota(jnp.int32, (HEAD_W, HEAD_W), 1)
    eye = r_i == c_i

    def to_col(x):
        return jnp.sum(jnp.where(eye, x, 0.0), axis=1, keepdims=True)

    for h in range(HEADS):
        sl = slice(h * HEAD_W, (h + 1) * HEAD_W)
        lb = _lower_bound(lb_ref, sl)
        f_all = lb + (1.0 - lb) * jax.nn.sigmoid(f_ref[:, sl])
        q_all = q_ref[:, sl]
        i_all = i_ref[:, sl]
        o_rows = []
        for r in range(rows):
            f_col = to_col(f_all[r:r + 1])
            q_col = to_col(q_all[r:r + 1])
            s_new = f_col * s_ref[r, h] + (1.0 - f_col) * i_all[r:r + 1]
            s_out_ref[r, h] = s_new
            o_rows.append(jnp.sum(q_col * s_new, axis=0, keepdims=True))
        o_ref[:, sl] = jnp.concatenate(o_rows, axis=0)


def _hgrn_sample(lb2, z, state, rows):
    nb = state.shape[0]
    width = HEADS * HEAD_W
    col = lambda off: pl.BlockSpec((rows, width), lambda i: (i, off))
    sspec = pl.BlockSpec((rows, HEADS, HEAD_W, HEAD_W), lambda i: (i, 0, 0, 0))
    return pl.pallas_call(
        functools.partial(_hgrn_sample_kernel, rows=rows),
        grid=(nb // rows,),
        in_specs=[pl.BlockSpec(lb2.shape, lambda i: (0, 0)), col(3), col(4), col(5), sspec],
        out_specs=[pl.BlockSpec((rows, width), lambda i: (i, 0)), sspec],
        out_shape=[jax.ShapeDtypeStruct((nb, width), F32),
                   jax.ShapeDtypeStruct(state.shape, F32)],
        compiler_params=_cparams(("parallel",)),
        name="hgrn_sample",
    )(lb2, z, z, z, state)


W_CHUNK = 512


def _head_rms(x, g):
    parts = []
    for h in range(HEADS):
        blk = x[:, h * HEAD_W:(h + 1) * HEAD_W]
        parts.append(_rms_rows(blk, g))
    return jnp.concatenate(parts, axis=1)


def _mixout_kernel(oa_ref, ob_ref, zg_ref, x_ref, na_ref, nb_ref, wa_hbm, wb_hbm, wo_hbm,
                   o_ref, wa_v, wb_v, wo_v, stage, sem):
    @pl.when(pl.program_id(0) == 0)
    def _():
        chunks = [(src, dst, r) for src, dst in ((wa_hbm, wa_v), (wb_hbm, wb_v), (wo_hbm, wo_v))
                  for r in range(0, src.shape[0], W_CHUNK)]

        def copy(c):
            src, _, r = chunks[c]
            return pltpu.make_async_copy(src.at[pl.ds(r, W_CHUNK), :], stage.at[c % 2], sem.at[c % 2])

        copy(0).start()
        for c in range(len(chunks)):
            if c + 1 < len(chunks):
                copy(c + 1).start()
            copy(c).wait()
            _, dst, r = chunks[c]
            dst[pl.ds(r, W_CHUNK), :] = stage[c % 2].astype(BF16)

    w = oa_ref.shape[1]
    n = o_ref.shape[1]
    a = (_head_rms(oa_ref[...], na_ref[...]) * (1.0 - LAM_INIT)).astype(BF16)
    og = zg_ref[:, :w].astype(F32)
    b = (_head_rms(ob_ref[...], nb_ref[...]) * (og * jax.nn.sigmoid(og))).astype(BF16)
    pa = jnp.dot(a, wa_v[...], preferred_element_type=F32)
    pb = jnp.dot(b, wb_v[...], preferred_element_type=F32)
    mg = (jax.nn.sigmoid(zg_ref[:, w:w + n].astype(F32)) * pa
          + jax.nn.sigmoid(zg_ref[:, w + n:].astype(F32)) * pb).astype(BF16)
    o_ref[...] = x_ref[...] + jnp.dot(mg, wo_v[...], preferred_element_type=F32)


def _mixout(oa, ob, zg, x, na, nb, wa, wb, wo, tm):
    m, w = oa.shape
    n = wo.shape[1]
    assert w % W_CHUNK == 0 and n % W_CHUNK == 0 and wa.shape[1] == n
    rows = lambda c: pl.BlockSpec((tm, c), lambda i: (i, 0))
    vec = pl.BlockSpec((1, HEAD_W), lambda i: (0, 0))
    anyspec = pl.BlockSpec(memory_space=pl.ANY)
    return pl.pallas_call(
        _mixout_kernel,
        grid=(m // tm,),
        in_specs=[rows(w), rows(w), rows(zg.shape[1]), rows(n), vec, vec, anyspec, anyspec, anyspec],
        out_specs=rows(n),
        out_shape=jax.ShapeDtypeStruct((m, n), F32),
        scratch_shapes=[pltpu.VMEM((w, n), BF16), pltpu.VMEM((w, n), BF16), pltpu.VMEM((n, n), BF16),
                        pltpu.VMEM((2, W_CHUNK, n), F32), pltpu.SemaphoreType.DMA((2,))],
        compiler_params=_cparams(("arbitrary",)),
        name="mixout",
    )(oa, ob, zg, x, na, nb, wa, wb, wo)


def _rope_tables(pos):
    half = HALF // 2
    inv = ROPE_THETA ** (-jnp.arange(half, dtype=F32) / half)
    ang = pos.astype(F32)[:, None] * inv[None, :]
    cos = jnp.tile(jnp.cos(ang), (1, LANES // half))
    sin = jnp.tile(jnp.sin(ang), (1, LANES // half))
    return cos, sin


def kernel(x_prompt, x_sample, cache_k, cache_v, state_hgrn, page_table, ffn1_norm, ffn1_w_gate, ffn1_w_up, ffn1_w_down, mix_norm, w_in, q_norm, k_norm, lambda_q1, lambda_k1, lambda_q2, lambda_k2, attn_sub_norm, hgrn_lower_bounds, hgrn_out_norm, w_proj_a, w_proj_b, w_out, ffn2_norm, ffn2_w_gate, ffn2_w_up, ffn2_w_down):
    batch, seq, d = x_prompt.shape
    nb = x_sample.shape[0]
    n_pages = page_table.shape[1]
    width = HEADS * HEAD_W
    n_a = 6 * width

    w1g, w1u, w1d = ffn1_w_gate[0], ffn1_w_up[0], ffn1_w_down[0]
    w2g, w2u, w2d = ffn2_w_gate[0], ffn2_w_up[0], ffn2_w_down[0]
    w_in_b = w_in[0]
    n_g = w_in_b.shape[1] - n_a
    wa, wb, wo = w_proj_a[0], w_proj_b[0], w_out[0]
    lam = jnp.concatenate([lambda_q1, lambda_k1, lambda_q2, lambda_k2], axis=0).astype(F32)
    gq = jnp.tile(q_norm.astype(F32), (1, 2))
    gk = jnp.tile(k_norm.astype(F32), (1, 2))
    lb2 = hgrn_lower_bounds.astype(F32)

    xp = x_prompt.reshape(batch * seq, d)
    xs = x_sample.reshape(nb, d)

    x1p, x1s = _ffn(xp, xs, ffn1_norm, w1g, w1u, w1d, 1024, 256)
    zap, zas = _normmm(x1p, x1s, mix_norm, w_in_b, 0, n_a, F32, 2048, 512)
    zgp, zgs = _normmm(x1p, x1s, mix_norm, w_in_b, n_a // 512, n_g, BF16, 2048, 512)
    cos_p, sin_p = _rope_tables(jnp.arange(seq))
    qb, kb, vb, kf, vf = _prep_prompt(zap, cos_p, sin_p, gq, gk, batch, seq, 256)
    pos_s = jnp.full((nb,), n_pages * PAGE, jnp.int32)
    cos_s, sin_s = _rope_tables(pos_s)
    qs, ks, vs = _prep_sample(zas, cos_s, sin_s, gq, gk)

    ck = cache_k.reshape(cache_k.shape[1], PAGE * HEADS, HEAD_W)
    cv = cache_v.reshape(cache_v.shape[1], PAGE * HEADS, HEAD_W)
    tiles = lambda a: a.reshape(nb, HEADS, HEAD_W)
    oa_p, oa_s = _attn(page_table, lam, qb, kb, vb, tiles(qs), tiles(ks), tiles(vs), ck, cv, 512)
    oa_s = oa_s.reshape(nb, width)
    ob_p, st_p = _hgrn_prompt(lb2, zap, batch, seq, 128)
    ob_s, st_s = _hgrn_sample(lb2, zas, state_hgrn[0], 8)

    x2p = _mixout(oa_p, ob_p, zgp, x1p, attn_sub_norm, hgrn_out_norm, wa, wb, wo, 256)
    x2s = _mixout(oa_s, ob_s, zgs, x1s, attn_sub_norm, hgrn_out_norm, wa, wb, wo, nb)
    y_p, y_s = _ffn(x2p, x2s, ffn2_norm, w2g, w2u, w2d, 1024, 256)

    return (y_p.reshape(batch, seq, d),
            y_s.reshape(nb, 1, d),
            kf.reshape(1, batch, seq, HEADS, HEAD_W),
            vf.reshape(1, batch, seq, HEADS, HEAD_W),
            ks.reshape(1, nb, 1, HEADS, HEAD_W),
            vs.reshape(1, nb, 1, HEADS, HEAD_W),
            st_p.reshape(1, batch, HEADS, HEAD_W, HEAD_W),
            st_s.reshape(1, nb, HEADS, HEAD_W, HEAD_W))
```

```python
import functools
import math

import jax
import jax.numpy as jnp
from jax import lax
from jax.experimental import pallas as pl
from jax.experimental.pallas import tpu as pltpu

F32 = jnp.float32
BF16 = jnp.bfloat16

EPS = 1e-6
ROPE_THETA = 10000.0
HEADS = 8
HEAD_W = 128
HALF = 64
PAGE = 128
LAM_INIT = 0.8 - 0.6 * math.exp(-0.3 * 0)
LANES = 128
VMEM_LIMIT = 56 * 1024 * 1024
NEG = -1e30


def _cparams(sem):
    return pltpu.CompilerParams(dimension_semantics=sem, vmem_limit_bytes=VMEM_LIMIT)


def _nt_dot(a, b):
    return lax.dot_general(a, b, (((1,), (1,)), ((), ())), preferred_element_type=F32)


def _rms_rows(x, g):
    ms = jnp.mean(x * x, axis=-1, keepdims=True)
    return x * lax.rsqrt(ms + EPS) * g


def _ffn_kernel(xp_ref, xs_ref, g_ref, wg_ref, wu_ref, wd_ref, op_ref, os_ref, h_ref):
    j = pl.program_id(1)
    tp = xp_ref.shape[0]

    @pl.when(j == 0)
    def _():
        xp = xp_ref[...]
        xs = xs_ref[...]
        g = g_ref[...]
        h_ref[:tp] = _rms_rows(xp, g).astype(BF16)
        h_ref[tp:] = _rms_rows(xs, g).astype(BF16)
        op_ref[...] = xp
        os_ref[...] = xs

    h = h_ref[...]
    a = jnp.dot(h, wg_ref[...].astype(BF16), preferred_element_type=F32)
    u = jnp.dot(h, wu_ref[...].astype(BF16), preferred_element_type=F32)
    t = (a * jax.nn.sigmoid(a) * (0.5 * u)).astype(BF16)
    y = jnp.dot(t, wd_ref[...].astype(BF16), preferred_element_type=F32)
    op_ref[...] += y[:tp]
    os_ref[...] += y[tp:]


def _ffn(xp, xs, g, wg, wu, wd, tm, tf):
    m, d = xp.shape
    f = wg.shape[1]
    nt = m // tm
    ts = xs.shape[0] // nt
    rows = lambda t: pl.BlockSpec((t, d), lambda i, j: (i, 0))
    acc = lambda t: pl.BlockSpec((t, d), lambda i, j: (i, 0), pipeline_mode=pl.Buffered(1))
    return pl.pallas_call(
        _ffn_kernel,
        grid=(nt, f // tf),
        in_specs=[
            rows(tm), rows(ts),
            pl.BlockSpec((1, d), lambda i, j: (0, 0)),
            pl.BlockSpec((d, tf), lambda i, j: (0, j)),
            pl.BlockSpec((d, tf), lambda i, j: (0, j)),
            pl.BlockSpec((tf, d), lambda i, j: (j, 0)),
        ],
        out_specs=[acc(tm), acc(ts)],
        out_shape=[jax.ShapeDtypeStruct(xp.shape, F32), jax.ShapeDtypeStruct(xs.shape, F32)],
        scratch_shapes=[pltpu.VMEM((tm + ts, d), BF16)],
        compiler_params=_cparams(("parallel", "arbitrary")),
        name="ffn",
    )(xp, xs, g, wg, wu, wd)


def _normmm_kernel(xp_ref, xs_ref, g_ref, w_ref, op_ref, os_ref, h_ref):
    tp = xp_ref.shape[0]

    @pl.when(pl.program_id(1) == 0)
    def _():
        g = g_ref[...]
        h_ref[:tp] = _rms_rows(xp_ref[...], g).astype(BF16)
        h_ref[tp:] = _rms_rows(xs_ref[...], g).astype(BF16)

    z = jnp.dot(h_ref[...], w_ref[...].astype(BF16), preferred_element_type=F32)
    op_ref[...] = z[:tp].astype(op_ref.dtype)
    os_ref[...] = z[tp:].astype(os_ref.dtype)


def _normmm(xp, xs, g, w, col_off, n, out_dtype, tm, tn):
    m, d = xp.shape
    nt = m // tm
    ts = xs.shape[0] // nt
    rows = lambda t: pl.BlockSpec((t, d), lambda i, j: (i, 0), pipeline_mode=pl.Buffered(1))
    return pl.pallas_call(
        _normmm_kernel,
        grid=(nt, n // tn),
        in_specs=[
            rows(tm), rows(ts),
            pl.BlockSpec((1, d), lambda i, j: (0, 0)),
            pl.BlockSpec((d, tn), lambda i, j: (0, col_off + j)),
        ],
        out_specs=[pl.BlockSpec((tm, tn), lambda i, j: (i, j)), pl.BlockSpec((ts, tn), lambda i, j: (i, j))],
        out_shape=[jax.ShapeDtypeStruct((m, n), out_dtype), jax.ShapeDtypeStruct((xs.shape[0], n), out_dtype)],
        scratch_shapes=[pltpu.VMEM((tm + ts, d), BF16)],
        compiler_params=_cparams(("parallel", "arbitrary")),
        name="normmm",
    )(xp, xs, g, w)


def _group_ms(x):
    r = lax.broadcasted_iota(jnp.int32, (LANES, LANES), 0)
    c = lax.broadcasted_iota(jnp.int32, (LANES, LANES), 1)
    grp = jnp.where((r >> 6) == (c >> 6), 1.0, 0.0).astype(BF16)
    x2 = x * x
    hi = x2.astype(BF16)
    r1 = x2 - hi.astype(F32)
    mid = r1.astype(BF16)
    lo = (r1 - mid.astype(F32)).astype(BF16)
    s = (jnp.dot(hi, grp, preferred_element_type=F32)
         + jnp.dot(mid, grp, preferred_element_type=F32)
         + jnp.dot(lo, grp, preferred_element_type=F32))
    return s * (1.0 / HALF)


def _norm_rope(x, g, cos, sin):
    y = x * lax.rsqrt(_group_ms(x) + EPS) * g
    lane = lax.broadcasted_iota(jnp.int32, y.shape, 1)
    first = (lane & (HALF - 1)) < (HALF // 2)
    rot = jnp.where(first, -pltpu.roll(y, LANES - HALF // 2, 1), pltpu.roll(y, HALF // 2, 1))
    return y * cos + rot * sin


def _prep_prompt_kernel(q_ref, k_ref, v_ref, cos_ref, sin_ref, gq_ref, gk_ref,
                        qb_ref, kb_ref, vb_ref, kf_ref, vf_ref):
    cos = cos_ref[...]
    sin = sin_ref[...]
    gq = gq_ref[...]
    gk = gk_ref[...]
    vf_ref[...] = v_ref[...]
    for h in range(HEADS):
        sl = slice(h * HEAD_W, (h + 1) * HEAD_W)
        k = _norm_rope(k_ref[:, sl], gk, cos, sin)
        kf_ref[:, sl] = k
        kb_ref[h] = k.astype(BF16)
        qb_ref[h] = (_norm_rope(q_ref[:, sl], gq, cos, sin) * (HALF ** -0.5)).astype(BF16)
        vb_ref[h] = v_ref[:, sl].astype(BF16)


def _prep_prompt(z, cos, sin, gq, gk, batch, seq, tm):
    nt = seq // tm
    width = HEADS * HEAD_W
    row = lambda b, i: b * nt + i
    hm = pl.BlockSpec((None, HEADS, tm, HEAD_W), lambda b, i: (b, 0, i, 0))
    flat = pl.BlockSpec((tm, width), lambda b, i: (row(b, i), 0))
    vec = pl.BlockSpec((1, HEAD_W), lambda b, i: (0, 0))
    tab = pl.BlockSpec((tm, HEAD_W), lambda b, i: (i, 0))
    hm_shape = jax.ShapeDtypeStruct((batch, HEADS, seq, HEAD_W), BF16)
    flat_shape = jax.ShapeDtypeStruct((batch * seq, width), F32)
    return pl.pallas_call(
        _prep_prompt_kernel,
        grid=(batch, nt),
        in_specs=[
            pl.BlockSpec((tm, width), lambda b, i: (row(b, i), 0)),
            pl.BlockSpec((tm, width), lambda b, i: (row(b, i), 1)),
            pl.BlockSpec((tm, width), lambda b, i: (row(b, i), 2)),
            tab, tab, vec, vec,
        ],
        out_specs=[hm, hm, hm, flat, flat],
        out_shape=[hm_shape, hm_shape, hm_shape, flat_shape, flat_shape],
        compiler_params=_cparams(("parallel", "parallel")),
        name="prep_prompt",
    )(z, z, z, cos, sin, gq, gk)


def _prep_sample_kernel(q_ref, k_ref, v_ref, cos_ref, sin_ref, gq_ref, gk_ref,
                        qf_ref, kf_ref, vf_ref):
    cos = cos_ref[...]
    sin = sin_ref[...]
    qf_ref[...] = _norm_rope(q_ref[...], gq_ref[...], cos, sin) * (HALF ** -0.5)
    kf_ref[...] = _norm_rope(k_ref[...], gk_ref[...], cos, sin)
    vf_ref[...] = v_ref[...]


def _prep_sample(z, cos, sin, gq, gk):
    m = z.shape[0]
    blk = lambda off: pl.BlockSpec((m, HEAD_W), lambda h: (0, off + h))
    vec = pl.BlockSpec((1, HEAD_W), lambda h: (0, 0))
    tab = pl.BlockSpec((m, HEAD_W), lambda h: (0, 0))
    shape = jax.ShapeDtypeStruct((m, HEADS * HEAD_W), F32)
    return pl.pallas_call(
        _prep_sample_kernel,
        grid=(HEADS,),
        in_specs=[blk(0), blk(HEADS), blk(2 * HEADS), tab, tab, vec, vec],
        out_specs=[blk(0), blk(0), blk(0)],
        out_shape=[shape, shape, shape],
        compiler_params=_cparams(("parallel",)),
        name="prep_sample",
    )(z, z, z, cos, sin, gq, gk)


def _lambda(lam_ref):
    l = lam_ref[...]
    s1 = jnp.sum(l[0:1] * l[1:2], axis=1, keepdims=True)
    s2 = jnp.sum(l[2:3] * l[3:4], axis=1, keepdims=True)
    return jnp.exp(s1) - jnp.exp(s2) + LAM_INIT


def _prompt_block(qi, lam, q_ref, k_ref, v_ref, o_ref, m_ref, l_ref, acc_ref, tq):
    q = q_ref[...]
    lane = lax.broadcasted_iota(jnp.int32, q.shape, 1)
    zero = jnp.zeros_like(q)
    qs = jnp.concatenate([jnp.where(lane < HALF, q, zero),
                          jnp.where(lane >= HALF, q, zero)], axis=0)

    m_ref[...] = jnp.full(m_ref.shape, NEG, F32)
    l_ref[...] = jnp.zeros(l_ref.shape, F32)
    acc_ref[...] = jnp.zeros(acc_ref.shape, F32)

    def step(kb, masked):
        start = pl.multiple_of(kb * tq, tq)
        k = k_ref[pl.ds(start, tq), :]
        v = v_ref[pl.ds(start, tq), :]
        s = _nt_dot(qs, k)
        if masked:
            r = lax.broadcasted_iota(jnp.int32, s.shape, 0)
            c = lax.broadcasted_iota(jnp.int32, s.shape, 1)
            r = jnp.where(r >= tq, r - tq, r)
            s = jnp.where(r >= c, s, NEG)
        m_prev = m_ref[...]
        m_new = jnp.maximum(m_prev, jnp.max(s, axis=1, keepdims=True))
        alpha = jnp.exp(m_prev - m_new)
        p = jnp.exp(s - jnp.concatenate([m_new] * (tq // LANES), axis=1))
        lsum = p[:, 0:LANES]
        for cblk in range(1, tq // LANES):
            lsum = lsum + p[:, cblk * LANES:(cblk + 1) * LANES]
        l_ref[...] = alpha * l_ref[...] + lsum
        acc_ref[...] = alpha * acc_ref[...] + jnp.dot(p.astype(BF16), v, preferred_element_type=F32)
        m_ref[...] = m_new

    def body(kb, carry):
        step(kb, False)
        return carry

    lax.fori_loop(0, qi, body, 0)
    step(qi, True)

    l = jnp.sum(l_ref[...], axis=1, keepdims=True)
    o = acc_ref[...] / l
    o_ref[...] = o[:tq] - lam * o[tq:]


def _sample_seq(lam, q, kn, vn, kpast, vpast):
    q2 = jnp.concatenate([q, q], axis=0)
    row = lax.broadcasted_iota(jnp.int32, q2.shape, 0)
    lane = lax.broadcasted_iota(jnp.int32, q2.shape, 1)
    qm = jnp.where((lane >> 6) == (row >> 3), q2, 0.0)

    s = _nt_dot(qm.astype(BF16), kpast.astype(BF16))
    own = ((lax.broadcasted_iota(jnp.int32, s.shape, 1) & (HEADS - 1))
           == (lax.broadcasted_iota(jnp.int32, s.shape, 0) & (HEADS - 1)))
    s = jnp.where(own, s, NEG)
    s_new = jnp.sum(qm * jnp.concatenate([kn, kn], axis=0), axis=1, keepdims=True)
    m = jnp.maximum(jnp.max(s, axis=1, keepdims=True), s_new)
    p = jnp.exp(s - m)
    p_new = jnp.exp(s_new - m)
    inv = 1.0 / (jnp.sum(p, axis=1, keepdims=True) + p_new)
    w = p * inv
    w_new = p_new * inv
    pd = w[:HEADS] - lam * w[HEADS:]
    pd_new = w_new[:HEADS] - lam * w_new[HEADS:]
    return jnp.dot(pd.astype(BF16), vpast.astype(BF16), preferred_element_type=F32) + pd_new * vn


def _causal_order(step, n):
    return jnp.where(step % 2 == 0, n - 1 - step // 2, step // 2)


def _attn_kernel(pt_ref, lam_ref, q_ref, k_ref, v_ref, qs_ref, kn_ref, vn_ref, ck_hbm, cv_hbm,
                 o_ref, os_ref, m_ref, l_ref, acc_ref, kbuf, vbuf, sem, *, tq, n_pages):
    step = pl.program_id(2)
    qi = _causal_order(step, pl.num_programs(2))
    seq = (pl.program_id(0) * pl.num_programs(1) + pl.program_id(1)) * pl.num_programs(2) + step
    n_seq = pl.num_programs(0) * pl.num_programs(1) * pl.num_programs(2)
    slot = seq % 2
    page_rows = PAGE * HEADS

    def page_copies(seq_, slot_):
        cps = []
        for p in range(n_pages):
            pg = pt_ref[seq_, p]
            dst = pl.ds(p * page_rows, page_rows)
            cps.append(pltpu.make_async_copy(ck_hbm.at[pg], kbuf.at[slot_, dst, :], sem.at[0, slot_]))
            cps.append(pltpu.make_async_copy(cv_hbm.at[pg], vbuf.at[slot_, dst, :], sem.at[1, slot_]))
        return cps

    @pl.when(seq == 0)
    def _():
        for cp in page_copies(0, 0):
            cp.start()

    @pl.when(seq + 1 < n_seq)
    def _():
        for cp in page_copies(seq + 1, 1 - slot):
            cp.start()

    lam = _lambda(lam_ref)
    _prompt_block(qi, lam, q_ref, k_ref, v_ref, o_ref, m_ref, l_ref, acc_ref, tq)

    for cp in page_copies(seq, slot):
        cp.wait()
    os_ref[...] = _sample_seq(lam, qs_ref[...], kn_ref[...], vn_ref[...], kbuf[slot], vbuf[slot])


def _attn(page_table, lam, qb, kb, vb, qs, kn, vn, ck, cv, tq):
    batch, heads, seq, w = qb.shape
    nq = seq // tq
    nb, n_pages = page_table.shape
    assert batch * heads * nq == nb, "one sample sequence per prompt query block"
    rows = n_pages * PAGE * HEADS
    sidx = lambda b, h, i, pt: ((b * heads + h) * nq + i, 0, 0)
    tile = pl.BlockSpec((None, HEADS, HEAD_W), sidx)
    kv = pl.BlockSpec((None, None, seq, w), lambda b, h, i, pt: (b, h, 0, 0))
    grid_spec = pltpu.PrefetchScalarGridSpec(
        num_scalar_prefetch=1,
        grid=(batch, heads, nq),
        in_specs=[
            pl.BlockSpec(lam.shape, lambda b, h, i, pt: (0, 0)),
            pl.BlockSpec((None, None, tq, w), lambda b, h, i, pt: (b, h, _causal_order(i, nq), 0)),
            kv, kv,
            tile, tile, tile,
            pl.BlockSpec(memory_space=pl.ANY),
            pl.BlockSpec(memory_space=pl.ANY),
        ],
        out_specs=[pl.BlockSpec((tq, w), lambda b, h, i, pt: (b * nq + _causal_order(i, nq), h)), tile],
        scratch_shapes=[pltpu.VMEM((2 * tq, LANES), F32),
                        pltpu.VMEM((2 * tq, LANES), F32),
                        pltpu.VMEM((2 * tq, w), F32),
                        pltpu.VMEM((2, rows, HEAD_W), F32),
                        pltpu.VMEM((2, rows, HEAD_W), F32),
                        pltpu.SemaphoreType.DMA((2, 2))],
    )
    return pl.pallas_call(
        functools.partial(_attn_kernel, tq=tq, n_pages=n_pages),
        grid_spec=grid_spec,
        out_shape=[jax.ShapeDtypeStruct((batch * seq, heads * w), F32),
                   jax.ShapeDtypeStruct((nb, HEADS, HEAD_W), F32)],
        compiler_params=_cparams(("arbitrary", "arbitrary", "arbitrary")),
        name="attn",
    )(page_table, lam, qb, kb, vb, qs, kn, vn, ck, cv)


def _lower_bound(lb_ref, sl):
    a = lb_ref[:, sl]
    mx = jnp.maximum(a[0:1], a[1:2])
    e0 = jnp.exp(a[0:1] - mx)
    e1 = jnp.exp(a[1:2] - mx)
    return e0 / (e0 + e1)


def _split3(x):
    hi = x.astype(BF16)
    r1 = x - hi.astype(F32)
    mid = r1.astype(BF16)
    lo = (r1 - mid.astype(F32)).astype(BF16)
    return hi, mid, lo


def _pair_ref(b, m, c):
    parts = [jnp.broadcast_to(b[2 * m * p + m - 1:2 * m * p + m, :], (2 * m, b.shape[1]))
             for p in range(c // (2 * m))]
    return parts[0] if len(parts) == 1 else jnp.concatenate(parts, axis=0)


def _hgrn_prompt_kernel(lb_ref, q_ref, f_ref, i_ref, o_ref, s_out_ref, st_ref, *, c):
    ci = pl.program_id(1)

    @pl.when(ci == 0)
    def _():
        st_ref[...] = jnp.zeros(st_ref.shape, F32)

    row = lax.broadcasted_iota(jnp.int32, (c, c), 0)
    col = lax.broadcasted_iota(jnp.int32, (c, c), 1)
    tri = jnp.where(col <= row, 1.0, 0.0).astype(BF16)
    same_sub = (row >> 4) == (col >> 4)
    causal = col <= row
    level_masks = {16: (row >> 5) == (col >> 5), 32: (row >> 6) == (col >> 6), 64: None}
    width = HEADS * HEAD_W
    trow = lax.broadcasted_iota(jnp.int32, (c, width), 0)
    sub = 16

    lb = _lower_bound(lb_ref, slice(None))
    q = q_ref[...]
    v = i_ref[...]
    f = lb + (1.0 - lb) * jax.nn.sigmoid(f_ref[...])
    k = 1.0 - f
    hi, mid, lo = _split3(jnp.log(f))
    b = (jnp.dot(tri, hi, preferred_element_type=F32)
         + jnp.dot(tri, mid, preferred_element_type=F32)
         + jnp.dot(tri, lo, preferred_element_type=F32))

    ref_d = jnp.concatenate(
        [jnp.zeros((sub, width), F32)]
        + [jnp.broadcast_to(b[sub * j - 1:sub * j, :], (sub, width)) for j in range(1, c // sub)],
        axis=0)
    arg_d = b - ref_d
    q_d = (q * jnp.exp(arg_d)).astype(BF16)
    k_d = (k * jnp.exp(-arg_d)).astype(BF16)
    levels = []
    for m in (16, 32, 64):
        odd = ((trow >> int(math.log2(m))) & 1) == 1
        d = b - _pair_ref(b, m, c)
        e = jnp.exp(jnp.where(odd, d, -d))
        levels.append((level_masks[m],
                       jnp.where(odd, q * e, 0.0).astype(BF16),
                       jnp.where(odd, 0.0, k * e).astype(BF16)))
    b_last = b[c - 1:c, :]
    q_t = (q * jnp.exp(b)).astype(BF16)
    k_t = (k * jnp.exp(b_last - b)).astype(BF16)
    decay = jnp.exp(b_last)
    v_b = v.astype(BF16)

    for h in range(HEADS):
        sl = slice(h * HEAD_W, (h + 1) * HEAD_W)
        att = jnp.where(same_sub, jnp.where(causal, _nt_dot(q_d[:, sl], k_d[:, sl]), 0.0), 0.0)
        for msk, q_m, k_m in levels:
            p_m = _nt_dot(q_m[:, sl], k_m[:, sl])
            att = att + (p_m if msk is None else jnp.where(msk, p_m, 0.0))
        st = st_ref[h]
        o_ref[:, sl] = (jnp.dot(att.astype(BF16), v_b[:, sl], preferred_element_type=F32)
                        + _nt_dot(q_t[:, sl], st.astype(BF16)))
        st_ref[h] = st * decay[:, sl] + jnp.dot(v[:, sl].T.astype(BF16), k_t[:, sl],
                                                preferred_element_type=F32)

    @pl.when(ci == pl.num_programs(1) - 1)
    def _():
        for h in range(HEADS):
            s_out_ref[h] = st_ref[h].T


def _hgrn_prompt(lb2, z, batch, seq, c):
    nc = seq // c
    width = HEADS * HEAD_W
    col = lambda off: pl.BlockSpec((c, width), lambda b, i: (b * nc + i, off))
    return pl.pallas_call(
        functools.partial(_hgrn_prompt_kernel, c=c),
        grid=(batch, nc),
        in_specs=[pl.BlockSpec(lb2.shape, lambda b, i: (0, 0)), col(3), col(4), col(5)],
        out_specs=[pl.BlockSpec((c, width), lambda b, i: (b * nc + i, 0)),
                   pl.BlockSpec((None, HEADS, HEAD_W, HEAD_W), lambda b, i: (b, 0, 0, 0))],
        out_shape=[jax.ShapeDtypeStruct((batch * seq, width), F32),
                   jax.ShapeDtypeStruct((batch, HEADS, HEAD_W, HEAD_W), F32)],
        scratch_shapes=[pltpu.VMEM((HEADS, HEAD_W, HEAD_W), F32)],
        compiler_params=_cparams(("parallel", "arbitrary")),
        name="hgrn_prompt",
    )(lb2, z, z, z)


def _hgrn_sample_kernel(lb_ref, q_ref, f_ref, i_ref, s_ref, o_ref, s_out_ref, *, rows):
    r_i = lax.broadcasted_iota(jnp.int32, (HEAD_W, HEAD_W), 0)
    c_i = lax.broadcasted_iota(jnp.int32, (HEAD_W, HEAD_W), 1)
    eye = r_i == c_i

    def to_col(x):
        return jnp.sum(jnp.where(eye, x, 0.0), axis=1, keepdims=True)

    for h in range(HEADS):
        sl = slice(h * HEAD_W, (h + 1) * HEAD_W)
        lb = _lower_bound(lb_ref, sl)
        f_all = lb + (1.0 - lb) * jax.nn.sigmoid(f_ref[:, sl])
        q_all = q_ref[:, sl]
        i_all = i_ref[:, sl]
        o_rows = []
        for r in range(rows):
            f_col = to_col(f_all[r:r + 1])
            q_col = to_col(q_all[r:r + 1])
            s_new = f_col * s_ref[r, h] + (1.0 - f_col) * i_all[r:r + 1]
            s_out_ref[r, h] = s_new
            o_rows.append(jnp.sum(q_col * s_new, axis=0, keepdims=True))
        o_ref[:, sl] = jnp.concatenate(o_rows, axis=0)


def _hgrn_sample(lb2, z, state, rows):
    nb = state.shape[0]
    width = HEADS * HEAD_W
    col = lambda off: pl.BlockSpec((rows, width), lambda i: (i, off))
    sspec = pl.BlockSpec((rows, HEADS, HEAD_W, HEAD_W), lambda i: (i, 0, 0, 0))
    return pl.pallas_call(
        functools.partial(_hgrn_sample_kernel, rows=rows),
        grid=(nb // rows,),
        in_specs=[pl.BlockSpec(lb2.shape, lambda i: (0, 0)), col(3), col(4), col(5), sspec],
        out_specs=[pl.BlockSpec((rows, width), lambda i: (i, 0)), sspec],
        out_shape=[jax.ShapeDtypeStruct((nb, width), F32),
                   jax.ShapeDtypeStruct(state.shape, F32)],
        compiler_params=_cparams(("parallel",)),
        name="hgrn_sample",
    )(lb2, z, z, z, state)


W_CHUNK = 512


def _head_rms(x, g):
    parts = []
    for h in range(HEADS):
        blk = x[:, h * HEAD_W:(h + 1) * HEAD_W]
        parts.append(_rms_rows(blk, g))
    return jnp.concatenate(parts, axis=1)


def _mixout_kernel(oa_ref, ob_ref, zg_ref, x_ref, na_ref, nb_ref, wa_hbm, wb_hbm, wo_hbm,
                   o_ref, wa_v, wb_v, wo_v, stage, sem):
    @pl.when(pl.program_id(0) == 0)
    def _():
        chunks = [(src, dst, r) for src, dst in ((wa_hbm, wa_v), (wb_hbm, wb_v), (wo_hbm, wo_v))
                  for r in range(0, src.shape[0], W_CHUNK)]

        def copy(c):
            src, _, r = chunks[c]
            return pltpu.make_async_copy(src.at[pl.ds(r, W_CHUNK), :], stage.at[c % 2], sem.at[c % 2])

        copy(0).start()
        for c in range(len(chunks)):
            if c + 1 < len(chunks):
                copy(c + 1).start()
            copy(c).wait()
            _, dst, r = chunks[c]
            dst[pl.ds(r, W_CHUNK), :] = stage[c % 2].astype(BF16)

    w = oa_ref.shape[1]
    n = o_ref.shape[1]
    a = (_head_rms(oa_ref[...], na_ref[...]) * (1.0 - LAM_INIT)).astype(BF16)
    og = zg_ref[:, :w].astype(F32)
    b = (_head_rms(ob_ref[...], nb_ref[...]) * (og * jax.nn.sigmoid(og))).astype(BF16)
    pa = jnp.dot(a, wa_v[...], preferred_element_type=F32)
    pb = jnp.dot(b, wb_v[...], preferred_element_type=F32)
    mg = (jax.nn.sigmoid(zg_ref[:, w:w + n].astype(F32)) * pa
          + jax.nn.sigmoid(zg_ref[:, w + n:].astype(F32)) * pb).astype(BF16)
    o_ref[...] = x_ref[...] + jnp.dot(mg, wo_v[...], preferred_element_type=F32)


def _mixout(oa, ob, zg, x, na, nb, wa, wb, wo, tm):
    m, w = oa.shape
    n = wo.shape[1]
    assert w % W_CHUNK == 0 and n % W_CHUNK == 0 and wa.shape[1] == n
    rows = lambda c: pl.BlockSpec((tm, c), lambda i: (i, 0))
    vec = pl.BlockSpec((1, HEAD_W), lambda i: (0, 0))
    anyspec = pl.BlockSpec(memory_space=pl.ANY)
    return pl.pallas_call(
        _mixout_kernel,
        grid=(m // tm,),
        in_specs=[rows(w), rows(w), rows(zg.shape[1]), rows(n), vec, vec, anyspec, anyspec, anyspec],
        out_specs=rows(n),
        out_shape=jax.ShapeDtypeStruct((m, n), F32),
        scratch_shapes=[pltpu.VMEM((w, n), BF16), pltpu.VMEM((w, n), BF16), pltpu.VMEM((n, n), BF16),
                        pltpu.VMEM((2, W_CHUNK, n), F32), pltpu.SemaphoreType.DMA((2,))],
        compiler_params=_cparams(("arbitrary",)),
        name="mixout",
    )(oa, ob, zg, x, na, nb, wa, wb, wo)


def _rope_tables(pos):
    half = HALF // 2
    inv = ROPE_THETA ** (-jnp.arange(half, dtype=F32) / half)
    ang = pos.astype(F32)[:, None] * inv[None, :]
    cos = jnp.tile(jnp.cos(ang), (1, LANES // half))
    sin = jnp.tile(jnp.sin(ang), (1, LANES // half))
    return cos, sin


def kernel(x_prompt, x_sample, cache_k, cache_v, state_hgrn, page_table, ffn1_norm, ffn1_w_gate, ffn1_w_up, ffn1_w_down, mix_norm, w_in, q_norm, k_norm, lambda_q1, lambda_k1, lambda_q2, lambda_k2, attn_sub_norm, hgrn_lower_bounds, hgrn_out_norm, w_proj_a, w_proj_b, w_out, ffn2_norm, ffn2_w_gate, ffn2_w_up, ffn2_w_down):
    batch, seq, d = x_prompt.shape
    nb = x_sample.shape[0]
    n_pages = page_table.shape[1]
    width = HEADS * HEAD_W
    n_a = 6 * width

    w1g, w1u, w1d = ffn1_w_gate[0], ffn1_w_up[0], ffn1_w_down[0]
    w2g, w2u, w2d = ffn2_w_gate[0], ffn2_w_up[0], ffn2_w_down[0]
    w_in_b = w_in[0]
    n_g = w_in_b.shape[1] - n_a
    wa, wb, wo = w_proj_a[0], w_proj_b[0], w_out[0]
    lam = jnp.concatenate([lambda_q1, lambda_k1, lambda_q2, lambda_k2], axis=0).astype(F32)
    gq = jnp.tile(q_norm.astype(F32), (1, 2))
    gk = jnp.tile(k_norm.astype(F32), (1, 2))
    lb2 = hgrn_lower_bounds.astype(F32)

    xp = x_prompt.reshape(batch * seq, d)
    xs = x_sample.reshape(nb, d)

    x1p, x1s = _ffn(xp, xs, ffn1_norm, w1g, w1u, w1d, 1024, 256)
    zap, zas = _normmm(x1p, x1s, mix_norm, w_in_b, 0, n_a, F32, 2048, 512)
    zgp, zgs = _normmm(x1p, x1s, mix_norm, w_in_b, n_a // 512, n_g, BF16, 2048, 512)
    cos_p, sin_p = _rope_tables(jnp.arange(seq))
    qb, kb, vb, kf, vf = _prep_prompt(zap, cos_p, sin_p, gq, gk, batch, seq, 256)
    pos_s = jnp.full((nb,), n_pages * PAGE, jnp.int32)
    cos_s, sin_s = _rope_tables(pos_s)
    qs, ks, vs = _prep_sample(zas, cos_s, sin_s, gq, gk)

    ck = cache_k.reshape(cache_k.shape[1], PAGE * HEADS, HEAD_W)
    cv = cache_v.reshape(cache_v.shape[1], PAGE * HEADS, HEAD_W)
    tiles = lambda a: a.reshape(nb, HEADS, HEAD_W)
    oa_p, oa_s = _attn(page_table, lam, qb, kb, vb, tiles(qs), tiles(ks), tiles(vs), ck, cv, 512)
    oa_s = oa_s.reshape(nb, width)
    ob_p, st_p = _hgrn_prompt(lb2, zap, batch, seq, 128)
    ob_s, st_s = _hgrn_sample(lb2, zas, state_hgrn[0], 8)

    x2p = _mixout(oa_p, ob_p, zgp, x1p, attn_sub_norm, hgrn_out_norm, wa, wb, wo, 256)
    x2s = _mixout(oa_s, ob_s, zgs, x1s, attn_sub_norm, hgrn_out_norm, wa, wb, wo, nb)
    y_p, y_s = _ffn(x2p, x2s, ffn2_norm, w2g, w2u, w2d, 1024, 256)

    return (y_p.reshape(batch, seq, d),
            y_s.reshape(nb, 1, d),
            kf.reshape(1, batch, seq, HEADS, HEAD_W),
            vf.reshape(1, batch, seq, HEADS, HEAD_W),
            ks.reshape(1, nb, 1, HEADS, HEAD_W),
            vs.reshape(1, nb, 1, HEADS, HEAD_W),
            st_p.reshape(1, batch, HEADS, HEAD_W, HEAD_W),
            st_s.reshape(1, nb, HEADS, HEAD_W, HEAD_W))
```

```python
import functools
import math

import jax
import jax.numpy as jnp
from jax import lax
from jax.experimental import pallas as pl
from jax.experimental.pallas import tpu as pltpu

F32 = jnp.float32
BF16 = jnp.bfloat16

EPS = 1e-6
ROPE_THETA = 10000.0
HEADS = 8
HEAD_W = 128
HALF = 64
PAGE = 128
LAM_INIT = 0.8 - 0.6 * math.exp(-0.3 * 0)
LANES = 128
VMEM_LIMIT = 56 * 1024 * 1024
NEG = -1e30


def _cparams(sem):
    return pltpu.CompilerParams(dimension_semantics=sem, vmem_limit_bytes=VMEM_LIMIT)


def _nt_dot(a, b):
    return lax.dot_general(a, b, (((1,), (1,)), ((), ())), preferred_element_type=F32)


def _rms_rows(x, g):
    ms = jnp.mean(x * x, axis=-1, keepdims=True)
    return x * lax.rsqrt(ms + EPS) * g


def _ffn_kernel(xp_ref, xs_ref, g_ref, wg_ref, wu_ref, wd_ref, op_ref, os_ref, h_ref):
    j = pl.program_id(1)
    tp = xp_ref.shape[0]

    @pl.when(j == 0)
    def _():
        xp = xp_ref[...]
        xs = xs_ref[...]
        g = g_ref[...]
        h_ref[:tp] = _rms_rows(xp, g).astype(BF16)
        h_ref[tp:] = _rms_rows(xs, g).astype(BF16)
        op_ref[...] = xp
        os_ref[...] = xs

    h = h_ref[...]
    a = jnp.dot(h, wg_ref[...].astype(BF16), preferred_element_type=F32)
    u = jnp.dot(h, wu_ref[...].astype(BF16), preferred_element_type=F32)
    t = (a * jax.nn.sigmoid(a) * (0.5 * u)).astype(BF16)
    y = jnp.dot(t, wd_ref[...].astype(BF16), preferred_element_type=F32)
    op_ref[...] += y[:tp]
    os_ref[...] += y[tp:]


def _ffn(xp, xs, g, wg, wu, wd, tm, tf):
    m, d = xp.shape
    f = wg.shape[1]
    nt = m // tm
    ts = xs.shape[0] // nt
    rows = lambda t: pl.BlockSpec((t, d), lambda i, j: (i, 0), pipeline_mode=pl.Buffered(1))
    acc = rows
    return pl.pallas_call(
        _ffn_kernel,
        grid=(nt, f // tf),
        in_specs=[
            rows(tm), rows(ts),
            pl.BlockSpec((1, d), lambda i, j: (0, 0)),
            pl.BlockSpec((d, tf), lambda i, j: (0, j)),
            pl.BlockSpec((d, tf), lambda i, j: (0, j)),
            pl.BlockSpec((tf, d), lambda i, j: (j, 0)),
        ],
        out_specs=[acc(tm), acc(ts)],
        out_shape=[jax.ShapeDtypeStruct(xp.shape, F32), jax.ShapeDtypeStruct(xs.shape, F32)],
        scratch_shapes=[pltpu.VMEM((tm + ts, d), BF16)],
        compiler_params=_cparams(("parallel", "arbitrary")),
        name="ffn",
    )(xp, xs, g, wg, wu, wd)


def _normmm_kernel(xp_ref, xs_ref, g_ref, w_ref, op_ref, os_ref, h_ref):
    tp = xp_ref.shape[0]

    @pl.when(pl.program_id(1) == 0)
    def _():
        g = g_ref[...]
        h_ref[:tp] = _rms_rows(xp_ref[...], g).astype(BF16)
        h_ref[tp:] = _rms_rows(xs_ref[...], g).astype(BF16)

    z = jnp.dot(h_ref[...], w_ref[...].astype(BF16), preferred_element_type=F32)
    op_ref[...] = z[:tp].astype(op_ref.dtype)
    os_ref[...] = z[tp:].astype(os_ref.dtype)


def _normmm(xp, xs, g, w, col_off, n, out_dtype, tm, tn):
    m, d = xp.shape
    nt = m // tm
    ts = xs.shape[0] // nt
    rows = lambda t: pl.BlockSpec((t, d), lambda i, j: (i, 0), pipeline_mode=pl.Buffered(1))
    return pl.pallas_call(
        _normmm_kernel,
        grid=(nt, n // tn),
        in_specs=[
            rows(tm), rows(ts),
            pl.BlockSpec((1, d), lambda i, j: (0, 0)),
            pl.BlockSpec((d, tn), lambda i, j: (0, col_off + j)),
        ],
        out_specs=[pl.BlockSpec((tm, tn), lambda i, j: (i, j)), pl.BlockSpec((ts, tn), lambda i, j: (i, j))],
        out_shape=[jax.ShapeDtypeStruct((m, n), out_dtype), jax.ShapeDtypeStruct((xs.shape[0], n), out_dtype)],
        scratch_shapes=[pltpu.VMEM((tm + ts, d), BF16)],
        compiler_params=_cparams(("parallel", "arbitrary")),
        name="normmm",
    )(xp, xs, g, w)


def _group_ms(x):
    r = lax.broadcasted_iota(jnp.int32, (LANES, LANES), 0)
    c = lax.broadcasted_iota(jnp.int32, (LANES, LANES), 1)
    grp = jnp.where((r >> 6) == (c >> 6), 1.0, 0.0).astype(BF16)
    x2 = x * x
    hi = x2.astype(BF16)
    r1 = x2 - hi.astype(F32)
    mid = r1.astype(BF16)
    lo = (r1 - mid.astype(F32)).astype(BF16)
    s = (jnp.dot(hi, grp, preferred_element_type=F32)
         + jnp.dot(mid, grp, preferred_element_type=F32)
         + jnp.dot(lo, grp, preferred_element_type=F32))
    return s * (1.0 / HALF)


def _norm_rope(x, g, cos, sin):
    y = x * lax.rsqrt(_group_ms(x) + EPS) * g
    lane = lax.broadcasted_iota(jnp.int32, y.shape, 1)
    first = (lane & (HALF - 1)) < (HALF // 2)
    rot = jnp.where(first, -pltpu.roll(y, LANES - HALF // 2, 1), pltpu.roll(y, HALF // 2, 1))
    return y * cos + rot * sin


def _prep_prompt_kernel(q_ref, k_ref, v_ref, cos_ref, sin_ref, gq_ref, gk_ref,
                        qb_ref, kb_ref, vb_ref, kf_ref, vf_ref):
    cos = cos_ref[...]
    sin = sin_ref[...]
    gq = gq_ref[...]
    gk = gk_ref[...]
    vf_ref[...] = v_ref[...]
    for h in range(HEADS):
        sl = slice(h * HEAD_W, (h + 1) * HEAD_W)
        k = _norm_rope(k_ref[:, sl], gk, cos, sin)
        kf_ref[:, sl] = k
        kb_ref[h] = k.astype(BF16)
        qb_ref[h] = (_norm_rope(q_ref[:, sl], gq, cos, sin) * (HALF ** -0.5)).astype(BF16)
        vb_ref[h] = v_ref[:, sl].astype(BF16)


def _prep_prompt(z, cos, sin, gq, gk, batch, seq, tm):
    nt = seq // tm
    width = HEADS * HEAD_W
    row = lambda b, i: b * nt + i
    hm = pl.BlockSpec((None, HEADS, tm, HEAD_W), lambda b, i: (b, 0, i, 0))
    flat = pl.BlockSpec((tm, width), lambda b, i: (row(b, i), 0))
    vec = pl.BlockSpec((1, HEAD_W), lambda b, i: (0, 0))
    tab = pl.BlockSpec((tm, HEAD_W), lambda b, i: (i, 0))
    hm_shape = jax.ShapeDtypeStruct((batch, HEADS, seq, HEAD_W), BF16)
    flat_shape = jax.ShapeDtypeStruct((batch * seq, width), F32)
    return pl.pallas_call(
        _prep_prompt_kernel,
        grid=(batch, nt),
        in_specs=[
            pl.BlockSpec((tm, width), lambda b, i: (row(b, i), 0)),
            pl.BlockSpec((tm, width), lambda b, i: (row(b, i), 1)),
            pl.BlockSpec((tm, width), lambda b, i: (row(b, i), 2)),
            tab, tab, vec, vec,
        ],
        out_specs=[hm, hm, hm, flat, flat],
        out_shape=[hm_shape, hm_shape, hm_shape, flat_shape, flat_shape],
        compiler_params=_cparams(("parallel", "parallel")),
        name="prep_prompt",
    )(z, z, z, cos, sin, gq, gk)


def _prep_sample_kernel(q_ref, k_ref, v_ref, cos_ref, sin_ref, gq_ref, gk_ref,
                        qf_ref, kf_ref, vf_ref):
    cos = cos_ref[...]
    sin = sin_ref[...]
    qf_ref[...] = _norm_rope(q_ref[...], gq_ref[...], cos, sin) * (HALF ** -0.5)
    kf_ref[...] = _norm_rope(k_ref[...], gk_ref[...], cos, sin)
    vf_ref[...] = v_ref[...]


def _prep_sample(z, cos, sin, gq, gk):
    m = z.shape[0]
    blk = lambda off: pl.BlockSpec((m, HEAD_W), lambda h: (0, off + h))
    vec = pl.BlockSpec((1, HEAD_W), lambda h: (0, 0))
    tab = pl.BlockSpec((m, HEAD_W), lambda h: (0, 0))
    shape = jax.ShapeDtypeStruct((m, HEADS * HEAD_W), F32)
    return pl.pallas_call(
        _prep_sample_kernel,
        grid=(HEADS,),
        in_specs=[blk(0), blk(HEADS), blk(2 * HEADS), tab, tab, vec, vec],
        out_specs=[blk(0), blk(0), blk(0)],
        out_shape=[shape, shape, shape],
        compiler_params=_cparams(("parallel",)),
        name="prep_sample",
    )(z, z, z, cos, sin, gq, gk)


def _lambda(lam_ref):
    l = lam_ref[...]
    s1 = jnp.sum(l[0:1] * l[1:2], axis=1, keepdims=True)
    s2 = jnp.sum(l[2:3] * l[3:4], axis=1, keepdims=True)
    return jnp.exp(s1) - jnp.exp(s2) + LAM_INIT


def _prompt_block(qi, lam, q_ref, k_ref, v_ref, o_ref, m_ref, l_ref, acc_ref, tq):
    q = q_ref[...]
    lane = lax.broadcasted_iota(jnp.int32, q.shape, 1)
    zero = jnp.zeros_like(q)
    qs = jnp.concatenate([jnp.where(lane < HALF, q, zero),
                          jnp.where(lane >= HALF, q, zero)], axis=0)

    m_ref[...] = jnp.full(m_ref.shape, NEG, F32)
    l_ref[...] = jnp.zeros(l_ref.shape, F32)
    acc_ref[...] = jnp.zeros(acc_ref.shape, F32)

    def step(kb, masked):
        start = pl.multiple_of(kb * tq, tq)
        k = k_ref[pl.ds(start, tq), :]
        v = v_ref[pl.ds(start, tq), :]
        s = _nt_dot(qs, k)
        if masked:
            r = lax.broadcasted_iota(jnp.int32, s.shape, 0)
            c = lax.broadcasted_iota(jnp.int32, s.shape, 1)
            r = jnp.where(r >= tq, r - tq, r)
            s = jnp.where(r >= c, s, NEG)
        m_prev = m_ref[...]
        m_new = jnp.maximum(m_prev, jnp.max(s, axis=1, keepdims=True))
        alpha = jnp.exp(m_prev - m_new)
        p = jnp.exp(s - jnp.concatenate([m_new] * (tq // LANES), axis=1))
        lsum = p[:, 0:LANES]
        for cblk in range(1, tq // LANES):
            lsum = lsum + p[:, cblk * LANES:(cblk + 1) * LANES]
        l_ref[...] = alpha * l_ref[...] + lsum
        acc_ref[...] = alpha * acc_ref[...] + jnp.dot(p.astype(BF16), v, preferred_element_type=F32)
        m_ref[...] = m_new

    def body(pair, carry):
        step(2 * pair, False)
        step(2 * pair + 1, False)
        return carry

    lax.fori_loop(0, qi // 2, body, 0)

    @pl.when(qi % 2 == 1)
    def _():
        step(qi - 1, False)

    step(qi, True)

    l = jnp.sum(l_ref[...], axis=1, keepdims=True)
    o = acc_ref[...] / l
    o_ref[...] = o[:tq] - lam * o[tq:]


def _sample_seq(lam, q, kn, vn, kpast, vpast):
    q2 = jnp.concatenate([q, q], axis=0)
    row = lax.broadcasted_iota(jnp.int32, q2.shape, 0)
    lane = lax.broadcasted_iota(jnp.int32, q2.shape, 1)
    qm = jnp.where((lane >> 6) == (row >> 3), q2, 0.0)

    s = _nt_dot(qm.astype(BF16), kpast.astype(BF16))
    own = ((lax.broadcasted_iota(jnp.int32, s.shape, 1) & (HEADS - 1))
           == (lax.broadcasted_iota(jnp.int32, s.shape, 0) & (HEADS - 1)))
    s = jnp.where(own, s, NEG)
    s_new = jnp.sum(qm * jnp.concatenate([kn, kn], axis=0), axis=1, keepdims=True)
    m = jnp.maximum(jnp.max(s, axis=1, keepdims=True), s_new)
    p = jnp.exp(s - m)
    p_new = jnp.exp(s_new - m)
    inv = 1.0 / (jnp.sum(p, axis=1, keepdims=True) + p_new)
    w = p * inv
    w_new = p_new * inv
    pd = w[:HEADS] - lam * w[HEADS:]
    pd_new = w_new[:HEADS] - lam * w_new[HEADS:]
    return jnp.dot(pd.astype(BF16), vpast.astype(BF16), preferred_element_type=F32) + pd_new * vn


def _causal_order(step, n):
    return jnp.where(step % 2 == 0, n - 1 - step // 2, step // 2)


def _attn_kernel(pt_ref, lam_ref, q_ref, k_ref, v_ref, qs_ref, kn_ref, vn_ref, ck_hbm, cv_hbm,
                 o_ref, os_ref, m_ref, l_ref, acc_ref, kbuf, vbuf, sem, *, tq, n_pages):
    step = pl.program_id(2)
    qi = _causal_order(step, pl.num_programs(2))
    seq = (pl.program_id(0) * pl.num_programs(1) + pl.program_id(1)) * pl.num_programs(2) + step
    n_seq = pl.num_programs(0) * pl.num_programs(1) * pl.num_programs(2)
    slot = seq % 2
    page_rows = PAGE * HEADS

    def page_copies(seq_, slot_):
        cps = []
        for p in range(n_pages):
            pg = pt_ref[seq_, p]
            dst = pl.ds(p * page_rows, page_rows)
            cps.append(pltpu.make_async_copy(ck_hbm.at[pg], kbuf.at[slot_, dst, :], sem.at[0, slot_]))
            cps.append(pltpu.make_async_copy(cv_hbm.at[pg], vbuf.at[slot_, dst, :], sem.at[1, slot_]))
        return cps

    @pl.when(seq == 0)
    def _():
        for cp in page_copies(0, 0):
            cp.start()

    @pl.when(seq + 1 < n_seq)
    def _():
        for cp in page_copies(seq + 1, 1 - slot):
            cp.start()

    lam = _lambda(lam_ref)
    _prompt_block(qi, lam, q_ref, k_ref, v_ref, o_ref, m_ref, l_ref, acc_ref, tq)

    for cp in page_copies(seq, slot):
        cp.wait()
    os_ref[...] = _sample_seq(lam, qs_ref[...], kn_ref[...], vn_ref[...], kbuf[slot], vbuf[slot])


def _attn(page_table, lam, qb, kb, vb, qs, kn, vn, ck, cv, tq):
    batch, heads, seq, w = qb.shape
    nq = seq // tq
    nb, n_pages = page_table.shape
    assert batch * heads * nq == nb, "one sample sequence per prompt query block"
    rows = n_pages * PAGE * HEADS
    sidx = lambda b, h, i, pt: ((b * heads + h) * nq + i, 0, 0)
    tile = pl.BlockSpec((None, HEADS, HEAD_W), sidx)
    kv = pl.BlockSpec((None, None, seq, w), lambda b, h, i, pt: (b, h, 0, 0))
    grid_spec = pltpu.PrefetchScalarGridSpec(
        num_scalar_prefetch=1,
        grid=(batch, heads, nq),
        in_specs=[
            pl.BlockSpec(lam.shape, lambda b, h, i, pt: (0, 0)),
            pl.BlockSpec((None, None, tq, w), lambda b, h, i, pt: (b, h, _causal_order(i, nq), 0)),
            kv, kv,
            tile, tile, tile,
            pl.BlockSpec(memory_space=pl.ANY),
            pl.BlockSpec(memory_space=pl.ANY),
        ],
        out_specs=[pl.BlockSpec((tq, w), lambda b, h, i, pt: (b * nq + _causal_order(i, nq), h)), tile],
        scratch_shapes=[pltpu.VMEM((2 * tq, LANES), F32),
                        pltpu.VMEM((2 * tq, LANES), F32),
                        pltpu.VMEM((2 * tq, w), F32),
                        pltpu.VMEM((2, rows, HEAD_W), F32),
                        pltpu.VMEM((2, rows, HEAD_W), F32),
                        pltpu.SemaphoreType.DMA((2, 2))],
    )
    return pl.pallas_call(
        functools.partial(_attn_kernel, tq=tq, n_pages=n_pages),
        grid_spec=grid_spec,
        out_shape=[jax.ShapeDtypeStruct((batch * seq, heads * w), F32),
                   jax.ShapeDtypeStruct((nb, HEADS, HEAD_W), F32)],
        compiler_params=_cparams(("arbitrary", "arbitrary", "arbitrary")),
        name="attn",
    )(page_table, lam, qb, kb, vb, qs, kn, vn, ck, cv)


def _lower_bound(lb_ref, sl):
    a = lb_ref[:, sl]
    mx = jnp.maximum(a[0:1], a[1:2])
    e0 = jnp.exp(a[0:1] - mx)
    e1 = jnp.exp(a[1:2] - mx)
    return e0 / (e0 + e1)


def _split3(x):
    hi = x.astype(BF16)
    r1 = x - hi.astype(F32)
    mid = r1.astype(BF16)
    lo = (r1 - mid.astype(F32)).astype(BF16)
    return hi, mid, lo


def _pair_ref(b, m, c):
    parts = [jnp.broadcast_to(b[2 * m * p + m - 1:2 * m * p + m, :], (2 * m, b.shape[1]))
             for p in range(c // (2 * m))]
    return parts[0] if len(parts) == 1 else jnp.concatenate(parts, axis=0)


def _hgrn_prompt_kernel(lb_ref, q_ref, f_ref, i_ref, o_ref, s_out_ref, st_ref, *, c):
    ci = pl.program_id(1)

    @pl.when(ci == 0)
    def _():
        st_ref[...] = jnp.zeros(st_ref.shape, F32)

    row = lax.broadcasted_iota(jnp.int32, (c, c), 0)
    col = lax.broadcasted_iota(jnp.int32, (c, c), 1)
    tri = jnp.where(col <= row, 1.0, 0.0).astype(BF16)
    same_sub = (row >> 4) == (col >> 4)
    causal = col <= row
    level_masks = {16: (row >> 5) == (col >> 5), 32: (row >> 6) == (col >> 6), 64: None}
    width = HEADS * HEAD_W
    trow = lax.broadcasted_iota(jnp.int32, (c, width), 0)
    sub = 16

    lb = _lower_bound(lb_ref, slice(None))
    q = q_ref[...]
    v = i_ref[...]
    f = lb + (1.0 - lb) * jax.nn.sigmoid(f_ref[...])
    k = 1.0 - f
    hi, mid, lo = _split3(jnp.log(f))
    b = (jnp.dot(tri, hi, preferred_element_type=F32)
         + jnp.dot(tri, mid, preferred_element_type=F32)
         + jnp.dot(tri, lo, preferred_element_type=F32))

    ref_d = jnp.concatenate(
        [jnp.zeros((sub, width), F32)]
        + [jnp.broadcast_to(b[sub * j - 1:sub * j, :], (sub, width)) for j in range(1, c // sub)],
        axis=0)
    arg_d = b - ref_d
    q_d = (q * jnp.exp(arg_d)).astype(BF16)
    k_d = (k * jnp.exp(-arg_d)).astype(BF16)
    levels = []
    for m in (16, 32, 64):
        odd = ((trow >> int(math.log2(m))) & 1) == 1
        d = b - _pair_ref(b, m, c)
        e = jnp.exp(jnp.where(odd, d, -d))
        levels.append((level_masks[m],
                       jnp.where(odd, q * e, 0.0).astype(BF16),
                       jnp.where(odd, 0.0, k * e).astype(BF16)))
    b_last = b[c - 1:c, :]
    q_t = (q * jnp.exp(b)).astype(BF16)
    k_t = (k * jnp.exp(b_last - b)).astype(BF16)
    decay = jnp.exp(b_last)
    v_b = v.astype(BF16)

    for h in range(HEADS):
        sl = slice(h * HEAD_W, (h + 1) * HEAD_W)
        att = jnp.where(same_sub, jnp.where(causal, _nt_dot(q_d[:, sl], k_d[:, sl]), 0.0), 0.0)
        for msk, q_m, k_m in levels:
            p_m = _nt_dot(q_m[:, sl], k_m[:, sl])
            att = att + (p_m if msk is None else jnp.where(msk, p_m, 0.0))
        st = st_ref[h]
        o_ref[:, sl] = (jnp.dot(att.astype(BF16), v_b[:, sl], preferred_element_type=F32)
                        + _nt_dot(q_t[:, sl], st.astype(BF16)))
        st_ref[h] = st * decay[:, sl] + jnp.dot(v[:, sl].T.astype(BF16), k_t[:, sl],
                                                preferred_element_type=F32)

    @pl.when(ci == pl.num_programs(1) - 1)
    def _():
        for h in range(HEADS):
            s_out_ref[h] = st_ref[h].T


def _hgrn_prompt(lb2, z, batch, seq, c):
    nc = seq // c
    width = HEADS * HEAD_W
    col = lambda off: pl.BlockSpec((c, width), lambda b, i: (b * nc + i, off))
    return pl.pallas_call(
        functools.partial(_hgrn_prompt_kernel, c=c),
        grid=(batch, nc),
        in_specs=[pl.BlockSpec(lb2.shape, lambda b, i: (0, 0)), col(3), col(4), col(5)],
        out_specs=[pl.BlockSpec((c, width), lambda b, i: (b * nc + i, 0)),
                   pl.BlockSpec((None, HEADS, HEAD_W, HEAD_W), lambda b, i: (b, 0, 0, 0))],
        out_shape=[jax.ShapeDtypeStruct((batch * seq, width), F32),
                   jax.ShapeDtypeStruct((batch, HEADS, HEAD_W, HEAD_W), F32)],
        scratch_shapes=[pltpu.VMEM((HEADS, HEAD_W, HEAD_W), F32)],
        compiler_params=_cparams(("parallel", "arbitrary")),
        name="hgrn_prompt",
    )(lb2, z, z, z)


def _hgrn_sample_kernel(lb_ref, q_ref, f_ref, i_ref, s_ref, o_ref, s_out_ref, *, rows):
    r_i = lax.broadcasted_iota(jnp.int32, (HEAD_W, HEAD_W), 0)
    c_i = lax.broadcasted_iota(jnp.int32, (HEAD_W, HEAD_W), 1)
    eye = r_i == c_i

    def to_col(x):
        return jnp.sum(jnp.where(eye, x, 0.0), axis=1, keepdims=True)

    for h in range(HEADS):
        sl = slice(h * HEAD_W, (h + 1) * HEAD_W)
        lb = _lower_bound(lb_ref, sl)
        f_all = lb + (1.0 - lb) * jax.nn.sigmoid(f_ref[:, sl])
        q_all = q_ref[:, sl]
        i_all = i_ref[:, sl]
        o_rows = []
        for r in range(rows):
            f_col = to_col(f_all[r:r + 1])
            q_col = to_col(q_all[r:r + 1])
            s_new = f_col * s_ref[r, h] + (1.0 - f_col) * i_all[r:r + 1]
            s_out_ref[r, h] = s_new
            o_rows.append(jnp.sum(q_col * s_new, axis=0, keepdims=True))
        o_ref[:, sl] = jnp.concatenate(o_rows, axis=0)


def _hgrn_sample(lb2, z, state, rows):
    nb = state.shape[0]
    width = HEADS * HEAD_W
    col = lambda off: pl.BlockSpec((rows, width), lambda i: (i, off))
    sspec = pl.BlockSpec((rows, HEADS, HEAD_W, HEAD_W), lambda i: (i, 0, 0, 0))
    return pl.pallas_call(
        functools.partial(_hgrn_sample_kernel, rows=rows),
        grid=(nb // rows,),
        in_specs=[pl.BlockSpec(lb2.shape, lambda i: (0, 0)), col(3), col(4), col(5), sspec],
        out_specs=[pl.BlockSpec((rows, width), lambda i: (i, 0)), sspec],
        out_shape=[jax.ShapeDtypeStruct((nb, width), F32),
                   jax.ShapeDtypeStruct(state.shape, F32)],
        compiler_params=_cparams(("parallel",)),
        name="hgrn_sample",
    )(lb2, z, z, z, state)


W_CHUNK = 512


def _head_rms(x, g):
    parts = []
    for h in range(HEADS):
        blk = x[:, h * HEAD_W:(h + 1) * HEAD_W]
        parts.append(_rms_rows(blk, g))
    return jnp.concatenate(parts, axis=1)


def _mixout_kernel(oa_ref, ob_ref, zg_ref, x_ref, na_ref, nb_ref, wa_hbm, wb_hbm, wo_hbm,
                   o_ref, wa_v, wb_v, wo_v, stage, sem):
    @pl.when(pl.program_id(0) == 0)
    def _():
        chunks = [(src, dst, r) for src, dst in ((wa_hbm, wa_v), (wb_hbm, wb_v), (wo_hbm, wo_v))
                  for r in range(0, src.shape[0], W_CHUNK)]

        def copy(c):
            src, _, r = chunks[c]
            return pltpu.make_async_copy(src.at[pl.ds(r, W_CHUNK), :], stage.at[c % 2], sem.at[c % 2])

        copy(0).start()
        for c in range(len(chunks)):
            if c + 1 < len(chunks):
                copy(c + 1).start()
            copy(c).wait()
            _, dst, r = chunks[c]
            dst[pl.ds(r, W_CHUNK), :] = stage[c % 2].astype(BF16)

    w = oa_ref.shape[1]
    n = o_ref.shape[1]
    a = (_head_rms(oa_ref[...], na_ref[...]) * (1.0 - LAM_INIT)).astype(BF16)
    og = zg_ref[:, :w].astype(F32)
    b = (_head_rms(ob_ref[...], nb_ref[...]) * (og * jax.nn.sigmoid(og))).astype(BF16)
    pa = jnp.dot(a, wa_v[...], preferred_element_type=F32)
    pb = jnp.dot(b, wb_v[...], preferred_element_type=F32)
    mg = (jax.nn.sigmoid(zg_ref[:, w:w + n].astype(F32)) * pa
          + jax.nn.sigmoid(zg_ref[:, w + n:].astype(F32)) * pb).astype(BF16)
    o_ref[...] = x_ref[...] + jnp.dot(mg, wo_v[...], preferred_element_type=F32)


def _mixout(oa, ob, zg, x, na, nb, wa, wb, wo, tm):
    m, w = oa.shape
    n = wo.shape[1]
    assert w % W_CHUNK == 0 and n % W_CHUNK == 0 and wa.shape[1] == n
    rows = lambda c: pl.BlockSpec((tm, c), lambda i: (i, 0))
    vec = pl.BlockSpec((1, HEAD_W), lambda i: (0, 0))
    anyspec = pl.BlockSpec(memory_space=pl.ANY)
    return pl.pallas_call(
        _mixout_kernel,
        grid=(m // tm,),
        in_specs=[rows(w), rows(w), rows(zg.shape[1]), rows(n), vec, vec, anyspec, anyspec, anyspec],
        out_specs=rows(n),
        out_shape=jax.ShapeDtypeStruct((m, n), F32),
        scratch_shapes=[pltpu.VMEM((w, n), BF16), pltpu.VMEM((w, n), BF16), pltpu.VMEM((n, n), BF16),
                        pltpu.VMEM((2, W_CHUNK, n), F32), pltpu.SemaphoreType.DMA((2,))],
        compiler_params=_cparams(("arbitrary",)),
        name="mixout",
    )(oa, ob, zg, x, na, nb, wa, wb, wo)


def _rope_tables(pos):
    half = HALF // 2
    inv = ROPE_THETA ** (-jnp.arange(half, dtype=F32) / half)
    ang = pos.astype(F32)[:, None] * inv[None, :]
    cos = jnp.tile(jnp.cos(ang), (1, LANES // half))
    sin = jnp.tile(jnp.sin(ang), (1, LANES // half))
    return cos, sin


def kernel(x_prompt, x_sample, cache_k, cache_v, state_hgrn, page_table, ffn1_norm, ffn1_w_gate, ffn1_w_up, ffn1_w_down, mix_norm, w_in, q_norm, k_norm, lambda_q1, lambda_k1, lambda_q2, lambda_k2, attn_sub_norm, hgrn_lower_bounds, hgrn_out_norm, w_proj_a, w_proj_b, w_out, ffn2_norm, ffn2_w_gate, ffn2_w_up, ffn2_w_down):
    batch, seq, d = x_prompt.shape
    nb = x_sample.shape[0]
    n_pages = page_table.shape[1]
    width = HEADS * HEAD_W
    n_a = 6 * width

    w1g, w1u, w1d = ffn1_w_gate[0], ffn1_w_up[0], ffn1_w_down[0]
    w2g, w2u, w2d = ffn2_w_gate[0], ffn2_w_up[0], ffn2_w_down[0]
    w_in_b = w_in[0]
    n_g = w_in_b.shape[1] - n_a
    wa, wb, wo = w_proj_a[0], w_proj_b[0], w_out[0]
    lam = jnp.concatenate([lambda_q1, lambda_k1, lambda_q2, lambda_k2], axis=0).astype(F32)
    gq = jnp.tile(q_norm.astype(F32), (1, 2))
    gk = jnp.tile(k_norm.astype(F32), (1, 2))
    lb2 = hgrn_lower_bounds.astype(F32)

    xp = x_prompt.reshape(batch * seq, d)
    xs = x_sample.reshape(nb, d)

    x1p, x1s = _ffn(xp, xs, ffn1_norm, w1g, w1u, w1d, 1024, 512)
    zap, zas = _normmm(x1p, x1s, mix_norm, w_in_b, 0, n_a, F32, 2048, 512)
    zgp, zgs = _normmm(x1p, x1s, mix_norm, w_in_b, n_a // 512, n_g, BF16, 2048, 512)
    cos_p, sin_p = _rope_tables(jnp.arange(seq))
    qb, kb, vb, kf, vf = _prep_prompt(zap, cos_p, sin_p, gq, gk, batch, seq, 256)
    pos_s = jnp.full((nb,), n_pages * PAGE, jnp.int32)
    cos_s, sin_s = _rope_tables(pos_s)
    qs, ks, vs = _prep_sample(zas, cos_s, sin_s, gq, gk)

    ck = cache_k.reshape(cache_k.shape[1], PAGE * HEADS, HEAD_W)
    cv = cache_v.reshape(cache_v.shape[1], PAGE * HEADS, HEAD_W)
    tiles = lambda a: a.reshape(nb, HEADS, HEAD_W)
    oa_p, oa_s = _attn(page_table, lam, qb, kb, vb, tiles(qs), tiles(ks), tiles(vs), ck, cv, 512)
    oa_s = oa_s.reshape(nb, width)
    ob_p, st_p = _hgrn_prompt(lb2, zap, batch, seq, 128)
    ob_s, st_s = _hgrn_sample(lb2, zas, state_hgrn[0], 8)

    x2p = _mixout(oa_p, ob_p, zgp, x1p, attn_sub_norm, hgrn_out_norm, wa, wb, wo, 256)
    x2s = _mixout(oa_s, ob_s, zgs, x1s, attn_sub_norm, hgrn_out_norm, wa, wb, wo, nb)
    y_p, y_s = _ffn(x2p, x2s, ffn2_norm, w2g, w2u, w2d, 1024, 512)

    return (y_p.reshape(batch, seq, d),
            y_s.reshape(nb, 1, d),
            kf.reshape(1, batch, seq, HEADS, HEAD_W),
            vf.reshape(1, batch, seq, HEADS, HEAD_W),
            ks.reshape(1, nb, 1, HEADS, HEAD_W),
            vs.reshape(1, nb, 1, HEADS, HEAD_W),
            st_p.reshape(1, batch, HEADS, HEAD_W, HEAD_W),
            st_s.reshape(1, nb, HEADS, HEAD_W, HEAD_W))
```

```python
import functools
import math

import jax
import jax.numpy as jnp
from jax import lax
from jax.experimental import pallas as pl
from jax.experimental.pallas import tpu as pltpu

F32 = jnp.float32
BF16 = jnp.bfloat16

EPS = 1e-6
ROPE_THETA = 10000.0
HEADS = 8
HEAD_W = 128
HALF = 64
PAGE = 128
LAM_INIT = 0.8 - 0.6 * math.exp(-0.3 * 0)
LANES = 128
VMEM_LIMIT = 56 * 1024 * 1024
NEG = -1e30


def _cparams(sem):
    return pltpu.CompilerParams(dimension_semantics=sem, vmem_limit_bytes=VMEM_LIMIT)


def _nt_dot(a, b):
    return lax.dot_general(a, b, (((1,), (1,)), ((), ())), preferred_element_type=F32)


def _rms_rows(x, g):
    ms = jnp.mean(x * x, axis=-1, keepdims=True)
    return x * lax.rsqrt(ms + EPS) * g


def _ffn_kernel(xp_ref, xs_ref, g_ref, wg_ref, wu_ref, wd_ref, op_ref, os_ref, h_ref):
    j = pl.program_id(1)
    tp = xp_ref.shape[0]

    @pl.when(j == 0)
    def _():
        xp = xp_ref[...]
        xs = xs_ref[...]
        g = g_ref[...]
        h_ref[:tp] = _rms_rows(xp, g).astype(BF16)
        h_ref[tp:] = _rms_rows(xs, g).astype(BF16)
        op_ref[...] = xp
        os_ref[...] = xs

    h = h_ref[...]
    a = jnp.dot(h, wg_ref[...].astype(BF16), preferred_element_type=F32)
    u = jnp.dot(h, wu_ref[...].astype(BF16), preferred_element_type=F32)
    t = (a * jax.nn.sigmoid(a) * (0.5 * u)).astype(BF16)
    y = jnp.dot(t, wd_ref[...].astype(BF16), preferred_element_type=F32)
    op_ref[...] += y[:tp]
    os_ref[...] += y[tp:]


def _ffn(xp, xs, g, wg, wu, wd, tm, tf):
    m, d = xp.shape
    f = wg.shape[1]
    nt = m // tm
    ts = xs.shape[0] // nt
    rows = lambda t: pl.BlockSpec((t, d), lambda i, j: (i, 0))
    acc = lambda t: pl.BlockSpec((t, d), lambda i, j: (i, 0), pipeline_mode=pl.Buffered(1))
    return pl.pallas_call(
        _ffn_kernel,
        grid=(nt, f // tf),
        in_specs=[
            rows(tm), rows(ts),
            pl.BlockSpec((1, d), lambda i, j: (0, 0)),
            pl.BlockSpec((d, tf), lambda i, j: (0, j)),
            pl.BlockSpec((d, tf), lambda i, j: (0, j)),
            pl.BlockSpec((tf, d), lambda i, j: (j, 0)),
        ],
        out_specs=[acc(tm), acc(ts)],
        out_shape=[jax.ShapeDtypeStruct(xp.shape, F32), jax.ShapeDtypeStruct(xs.shape, F32)],
        scratch_shapes=[pltpu.VMEM((tm + ts, d), BF16)],
        compiler_params=_cparams(("parallel", "arbitrary")),
        name="ffn",
    )(xp, xs, g, wg, wu, wd)


def _normmm_kernel(xp_ref, xs_ref, g_ref, w_ref, ap_ref, as_ref, gp_ref, gs_ref, h_ref, *, a_tiles):
    j = pl.program_id(1)
    tp = xp_ref.shape[0]

    @pl.when(j == 0)
    def _():
        g = g_ref[...]
        h_ref[:tp] = _rms_rows(xp_ref[...], g).astype(BF16)
        h_ref[tp:] = _rms_rows(xs_ref[...], g).astype(BF16)

    z = jnp.dot(h_ref[...], w_ref[...].astype(BF16), preferred_element_type=F32)

    @pl.when(j < a_tiles)
    def _():
        ap_ref[...] = z[:tp]
        as_ref[...] = z[tp:]

    @pl.when(j >= a_tiles)
    def _():
        gp_ref[...] = z[:tp].astype(BF16)
        gs_ref[...] = z[tp:].astype(BF16)


def _normmm(xp, xs, g, w, n_a, tm, tn):
    m, d = xp.shape
    n = w.shape[1]
    nt = m // tm
    ts = xs.shape[0] // nt
    a_tiles = n_a // tn
    rows = lambda t: pl.BlockSpec((t, d), lambda i, j: (i, 0), pipeline_mode=pl.Buffered(1))
    a_out = lambda t: pl.BlockSpec((t, tn), lambda i, j: (i, jnp.minimum(j, a_tiles - 1)))
    g_out = lambda t: pl.BlockSpec((t, tn), lambda i, j: (i, jnp.maximum(j - a_tiles, 0)))
    return pl.pallas_call(
        functools.partial(_normmm_kernel, a_tiles=a_tiles),
        grid=(nt, n // tn),
        in_specs=[
            rows(tm), rows(ts),
            pl.BlockSpec((1, d), lambda i, j: (0, 0)),
            pl.BlockSpec((d, tn), lambda i, j: (0, j)),
        ],
        out_specs=[a_out(tm), a_out(ts), g_out(tm), g_out(ts)],
        out_shape=[jax.ShapeDtypeStruct((m, n_a), F32), jax.ShapeDtypeStruct((xs.shape[0], n_a), F32),
                   jax.ShapeDtypeStruct((m, n - n_a), BF16), jax.ShapeDtypeStruct((xs.shape[0], n - n_a), BF16)],
        scratch_shapes=[pltpu.VMEM((tm + ts, d), BF16)],
        compiler_params=_cparams(("parallel", "arbitrary")),
        name="normmm",
    )(xp, xs, g, w)


def _group_ms(x):
    r = lax.broadcasted_iota(jnp.int32, (LANES, LANES), 0)
    c = lax.broadcasted_iota(jnp.int32, (LANES, LANES), 1)
    grp = jnp.where((r >> 6) == (c >> 6), 1.0, 0.0).astype(BF16)
    x2 = x * x
    hi = x2.astype(BF16)
    r1 = x2 - hi.astype(F32)
    mid = r1.astype(BF16)
    lo = (r1 - mid.astype(F32)).astype(BF16)
    s = (jnp.dot(hi, grp, preferred_element_type=F32)
         + jnp.dot(mid, grp, preferred_element_type=F32)
         + jnp.dot(lo, grp, preferred_element_type=F32))
    return s * (1.0 / HALF)


def _norm_rope(x, g, cos, sin):
    y = x * lax.rsqrt(_group_ms(x) + EPS) * g
    lane = lax.broadcasted_iota(jnp.int32, y.shape, 1)
    first = (lane & (HALF - 1)) < (HALF // 2)
    rot = jnp.where(first, -pltpu.roll(y, LANES - HALF // 2, 1), pltpu.roll(y, HALF // 2, 1))
    return y * cos + rot * sin


def _prep_prompt_kernel(q_ref, k_ref, v_ref, cos_ref, sin_ref, gq_ref, gk_ref,
                        qb_ref, kb_ref, vb_ref, kf_ref, vf_ref):
    cos = cos_ref[...]
    sin = sin_ref[...]
    gq = gq_ref[...]
    gk = gk_ref[...]
    vf_ref[...] = v_ref[...]
    for h in range(HEADS):
        sl = slice(h * HEAD_W, (h + 1) * HEAD_W)
        k = _norm_rope(k_ref[:, sl], gk, cos, sin)
        kf_ref[:, sl] = k
        kb_ref[h] = k.astype(BF16)
        qb_ref[h] = (_norm_rope(q_ref[:, sl], gq, cos, sin) * (HALF ** -0.5)).astype(BF16)
        vb_ref[h] = v_ref[:, sl].astype(BF16)


def _prep_prompt(z, cos, sin, gq, gk, batch, seq, tm):
    nt = seq // tm
    width = HEADS * HEAD_W
    row = lambda b, i: b * nt + i
    hm = pl.BlockSpec((None, HEADS, tm, HEAD_W), lambda b, i: (b, 0, i, 0))
    flat = pl.BlockSpec((tm, width), lambda b, i: (row(b, i), 0))
    vec = pl.BlockSpec((1, HEAD_W), lambda b, i: (0, 0))
    tab = pl.BlockSpec((tm, HEAD_W), lambda b, i: (i, 0))
    hm_shape = jax.ShapeDtypeStruct((batch, HEADS, seq, HEAD_W), BF16)
    flat_shape = jax.ShapeDtypeStruct((batch * seq, width), F32)
    return pl.pallas_call(
        _prep_prompt_kernel,
        grid=(batch, nt),
        in_specs=[
            pl.BlockSpec((tm, width), lambda b, i: (row(b, i), 0)),
            pl.BlockSpec((tm, width), lambda b, i: (row(b, i), 1)),
            pl.BlockSpec((tm, width), lambda b, i: (row(b, i), 2)),
            tab, tab, vec, vec,
        ],
        out_specs=[hm, hm, hm, flat, flat],
        out_shape=[hm_shape, hm_shape, hm_shape, flat_shape, flat_shape],
        compiler_params=_cparams(("parallel", "parallel")),
        name="prep_prompt",
    )(z, z, z, cos, sin, gq, gk)


def _prep_sample_kernel(q_ref, k_ref, v_ref, cos_ref, sin_ref, gq_ref, gk_ref,
                        qf_ref, kf_ref, vf_ref):
    cos = cos_ref[...]
    sin = sin_ref[...]
    qf_ref[...] = _norm_rope(q_ref[...], gq_ref[...], cos, sin) * (HALF ** -0.5)
    kf_ref[...] = _norm_rope(k_ref[...], gk_ref[...], cos, sin)
    vf_ref[...] = v_ref[...]


def _prep_sample(z, cos, sin, gq, gk):
    m = z.shape[0]
    blk = lambda off: pl.BlockSpec((m, HEAD_W), lambda h: (0, off + h))
    vec = pl.BlockSpec((1, HEAD_W), lambda h: (0, 0))
    tab = pl.BlockSpec((m, HEAD_W), lambda h: (0, 0))
    shape = jax.ShapeDtypeStruct((m, HEADS * HEAD_W), F32)
    return pl.pallas_call(
        _prep_sample_kernel,
        grid=(HEADS,),
        in_specs=[blk(0), blk(HEADS), blk(2 * HEADS), tab, tab, vec, vec],
        out_specs=[blk(0), blk(0), blk(0)],
        out_shape=[shape, shape, shape],
        compiler_params=_cparams(("parallel",)),
        name="prep_sample",
    )(z, z, z, cos, sin, gq, gk)


def _lambda(lam_ref):
    l = lam_ref[...]
    s1 = jnp.sum(l[0:1] * l[1:2], axis=1, keepdims=True)
    s2 = jnp.sum(l[2:3] * l[3:4], axis=1, keepdims=True)
    return jnp.exp(s1) - jnp.exp(s2) + LAM_INIT


def _prompt_block(qi, lam, q_ref, k_ref, v_ref, o_ref, m_ref, l_ref, acc_ref, tq):
    q = q_ref[...]
    lane = lax.broadcasted_iota(jnp.int32, q.shape, 1)
    zero = jnp.zeros_like(q)
    qs = jnp.concatenate([jnp.where(lane < HALF, q, zero),
                          jnp.where(lane >= HALF, q, zero)], axis=0)

    m_ref[...] = jnp.full(m_ref.shape, NEG, F32)
    l_ref[...] = jnp.zeros(l_ref.shape, F32)
    acc_ref[...] = jnp.zeros(acc_ref.shape, F32)

    def step(kb, masked):
        start = pl.multiple_of(kb * tq, tq)
        k = k_ref[pl.ds(start, tq), :]
        v = v_ref[pl.ds(start, tq), :]
        s = _nt_dot(qs, k)
        if masked:
            r = lax.broadcasted_iota(jnp.int32, s.shape, 0)
            c = lax.broadcasted_iota(jnp.int32, s.shape, 1)
            r = jnp.where(r >= tq, r - tq, r)
            s = jnp.where(r >= c, s, NEG)
        m_prev = m_ref[...]
        m_new = jnp.maximum(m_prev, jnp.max(s, axis=1, keepdims=True))
        alpha = jnp.exp(m_prev - m_new)
        p = jnp.exp(s - jnp.concatenate([m_new] * (tq // LANES), axis=1))
        lsum = p[:, 0:LANES]
        for cblk in range(1, tq // LANES):
            lsum = lsum + p[:, cblk * LANES:(cblk + 1) * LANES]
        l_ref[...] = alpha * l_ref[...] + lsum
        acc_ref[...] = alpha * acc_ref[...] + jnp.dot(p.astype(BF16), v, preferred_element_type=F32)
        m_ref[...] = m_new

    def body(pair, carry):
        step(2 * pair, False)
        step(2 * pair + 1, False)
        return carry

    lax.fori_loop(0, qi // 2, body, 0)

    @pl.when(qi % 2 == 1)
    def _():
        step(qi - 1, False)

    step(qi, True)

    l = jnp.sum(l_ref[...], axis=1, keepdims=True)
    o = acc_ref[...] / l
    o_ref[...] = o[:tq] - lam * o[tq:]


def _sample_seq(lam, q, kn, vn, kpast, vpast):
    q2 = jnp.concatenate([q, q], axis=0)
    row = lax.broadcasted_iota(jnp.int32, q2.shape, 0)
    lane = lax.broadcasted_iota(jnp.int32, q2.shape, 1)
    qm = jnp.where((lane >> 6) == (row >> 3), q2, 0.0)

    s = _nt_dot(qm.astype(BF16), kpast.astype(BF16))
    own = ((lax.broadcasted_iota(jnp.int32, s.shape, 1) & (HEADS - 1))
           == (lax.broadcasted_iota(jnp.int32, s.shape, 0) & (HEADS - 1)))
    s = jnp.where(own, s, NEG)
    s_new = jnp.sum(qm * jnp.concatenate([kn, kn], axis=0), axis=1, keepdims=True)
    m = jnp.maximum(jnp.max(s, axis=1, keepdims=True), s_new)
    p = jnp.exp(s - m)
    p_new = jnp.exp(s_new - m)
    inv = 1.0 / (jnp.sum(p, axis=1, keepdims=True) + p_new)
    c0 = inv[:HEADS]
    c1 = lam * inv[HEADS:]
    pd = p[:HEADS] * c0 - p[HEADS:] * c1
    pd_new = p_new[:HEADS] * c0 - p_new[HEADS:] * c1
    return jnp.dot(pd.astype(BF16), vpast.astype(BF16), preferred_element_type=F32) + pd_new * vn


def _causal_order(step, n):
    return jnp.where(step % 2 == 0, n - 1 - step // 2, step // 2)


def _attn_kernel(pt_ref, lam_ref, q_ref, k_ref, v_ref, qs_ref, kn_ref, vn_ref, ck_hbm, cv_hbm,
                 o_ref, os_ref, m_ref, l_ref, acc_ref, kbuf, vbuf, sem, *, tq, n_pages):
    step = pl.program_id(2)
    qi = _causal_order(step, pl.num_programs(2))
    seq = (pl.program_id(0) * pl.num_programs(1) + pl.program_id(1)) * pl.num_programs(2) + step
    n_seq = pl.num_programs(0) * pl.num_programs(1) * pl.num_programs(2)
    slot = seq % 2
    page_rows = PAGE * HEADS

    def page_copies(seq_, slot_):
        cps = []
        for p in range(n_pages):
            pg = pt_ref[seq_, p]
            dst = pl.ds(p * page_rows, page_rows)
            cps.append(pltpu.make_async_copy(ck_hbm.at[pg], kbuf.at[slot_, dst, :], sem.at[0, slot_]))
            cps.append(pltpu.make_async_copy(cv_hbm.at[pg], vbuf.at[slot_, dst, :], sem.at[1, slot_]))
        return cps

    @pl.when(seq == 0)
    def _():
        for cp in page_copies(0, 0):
            cp.start()

    @pl.when(seq + 1 < n_seq)
    def _():
        for cp in page_copies(seq + 1, 1 - slot):
            cp.start()

    lam = _lambda(lam_ref)
    _prompt_block(qi, lam, q_ref, k_ref, v_ref, o_ref, m_ref, l_ref, acc_ref, tq)

    for cp in page_copies(seq, slot):
        cp.wait()
    os_ref[...] = _sample_seq(lam, qs_ref[...], kn_ref[...], vn_ref[...], kbuf[slot], vbuf[slot])


def _attn(page_table, lam, qb, kb, vb, qs, kn, vn, ck, cv, tq):
    batch, heads, seq, w = qb.shape
    nq = seq // tq
    nb, n_pages = page_table.shape
    assert batch * heads * nq == nb, "one sample sequence per prompt query block"
    rows = n_pages * PAGE * HEADS
    sidx = lambda b, h, i, pt: ((b * heads + h) * nq + i, 0, 0)
    tile = pl.BlockSpec((None, HEADS, HEAD_W), sidx)
    kv = pl.BlockSpec((None, None, seq, w), lambda b, h, i, pt: (b, h, 0, 0))
    grid_spec = pltpu.PrefetchScalarGridSpec(
        num_scalar_prefetch=1,
        grid=(batch, heads, nq),
        in_specs=[
            pl.BlockSpec(lam.shape, lambda b, h, i, pt: (0, 0)),
            pl.BlockSpec((None, None, tq, w), lambda b, h, i, pt: (b, h, _causal_order(i, nq), 0)),
            kv, kv,
            tile, tile, tile,
            pl.BlockSpec(memory_space=pl.ANY),
            pl.BlockSpec(memory_space=pl.ANY),
        ],
        out_specs=[pl.BlockSpec((tq, w), lambda b, h, i, pt: (b * nq + _causal_order(i, nq), h)), tile],
        scratch_shapes=[pltpu.VMEM((2 * tq, LANES), F32),
                        pltpu.VMEM((2 * tq, LANES), F32),
                        pltpu.VMEM((2 * tq, w), F32),
                        pltpu.VMEM((2, rows, HEAD_W), F32),
                        pltpu.VMEM((2, rows, HEAD_W), F32),
                        pltpu.SemaphoreType.DMA((2, 2))],
    )
    return pl.pallas_call(
        functools.partial(_attn_kernel, tq=tq, n_pages=n_pages),
        grid_spec=grid_spec,
        out_shape=[jax.ShapeDtypeStruct((batch * seq, heads * w), F32),
                   jax.ShapeDtypeStruct((nb, HEADS, HEAD_W), F32)],
        compiler_params=_cparams(("arbitrary", "arbitrary", "arbitrary")),
        name="attn",
    )(page_table, lam, qb, kb, vb, qs, kn, vn, ck, cv)


def _lower_bound(lb_ref, sl):
    a = lb_ref[:, sl]
    mx = jnp.maximum(a[0:1], a[1:2])
    e0 = jnp.exp(a[0:1] - mx)
    e1 = jnp.exp(a[1:2] - mx)
    return e0 / (e0 + e1)


def _split3(x):
    hi = x.astype(BF16)
    r1 = x - hi.astype(F32)
    mid = r1.astype(BF16)
    lo = (r1 - mid.astype(F32)).astype(BF16)
    return hi, mid, lo


def _pair_ref(b, m, c):
    parts = [jnp.broadcast_to(b[2 * m * p + m - 1:2 * m * p + m, :], (2 * m, b.shape[1]))
             for p in range(c // (2 * m))]
    return parts[0] if len(parts) == 1 else jnp.concatenate(parts, axis=0)


def _hgrn_prompt_kernel(lb_ref, q_ref, f_ref, i_ref, o_ref, s_out_ref, st_ref, *, c):
    ci = pl.program_id(1)

    @pl.when(ci == 0)
    def _():
        st_ref[...] = jnp.zeros(st_ref.shape, F32)

    row = lax.broadcasted_iota(jnp.int32, (c, c), 0)
    col = lax.broadcasted_iota(jnp.int32, (c, c), 1)
    tri = jnp.where(col <= row, 1.0, 0.0).astype(BF16)
    same_sub = (row >> 4) == (col >> 4)
    causal = col <= row
    level_masks = {16: (row >> 5) == (col >> 5), 32: (row >> 6) == (col >> 6), 64: None}
    width = HEADS * HEAD_W
    trow = lax.broadcasted_iota(jnp.int32, (c, width), 0)
    sub = 16

    lb = _lower_bound(lb_ref, slice(None))
    q = q_ref[...]
    v = i_ref[...]
    f = lb + (1.0 - lb) * jax.nn.sigmoid(f_ref[...])
    k = 1.0 - f
    hi, mid, lo = _split3(jnp.log(f))
    b = (jnp.dot(tri, hi, preferred_element_type=F32)
         + jnp.dot(tri, mid, preferred_element_type=F32)
         + jnp.dot(tri, lo, preferred_element_type=F32))

    ref_d = jnp.concatenate(
        [jnp.zeros((sub, width), F32)]
        + [jnp.broadcast_to(b[sub * j - 1:sub * j, :], (sub, width)) for j in range(1, c // sub)],
        axis=0)
    arg_d = b - ref_d
    q_d = (q * jnp.exp(arg_d)).astype(BF16)
    k_d = (k * jnp.exp(-arg_d)).astype(BF16)
    levels = []
    for m in (16, 32, 64):
        odd = ((trow >> int(math.log2(m))) & 1) == 1
        d = b - _pair_ref(b, m, c)
        e = jnp.exp(jnp.where(odd, d, -d))
        levels.append((level_masks[m],
                       jnp.where(odd, q * e, 0.0).astype(BF16),
                       jnp.where(odd, 0.0, k * e).astype(BF16)))
    b_last = b[c - 1:c, :]
    q_t = (q * jnp.exp(b)).astype(BF16)
    k_t = (k * jnp.exp(b_last - b)).astype(BF16)
    decay = jnp.exp(b_last)
    v_b = v.astype(BF16)

    for h in range(HEADS):
        sl = slice(h * HEAD_W, (h + 1) * HEAD_W)
        att = jnp.where(same_sub, jnp.where(causal, _nt_dot(q_d[:, sl], k_d[:, sl]), 0.0), 0.0)
        for msk, q_m, k_m in levels:
            p_m = _nt_dot(q_m[:, sl], k_m[:, sl])
            att = att + (p_m if msk is None else jnp.where(msk, p_m, 0.0))
        st = st_ref[h]
        o_ref[:, sl] = (jnp.dot(att.astype(BF16), v_b[:, sl], preferred_element_type=F32)
                        + _nt_dot(q_t[:, sl], st.astype(BF16)))
        st_ref[h] = st * decay[:, sl] + jnp.dot(v[:, sl].T.astype(BF16), k_t[:, sl],
                                                preferred_element_type=F32)

    @pl.when(ci == pl.num_programs(1) - 1)
    def _():
        for h in range(HEADS):
            s_out_ref[h] = st_ref[h].T


def _hgrn_prompt(lb2, z, batch, seq, c):
    nc = seq // c
    width = HEADS * HEAD_W
    col = lambda off: pl.BlockSpec((c, width), lambda b, i: (b * nc + i, off))
    return pl.pallas_call(
        functools.partial(_hgrn_prompt_kernel, c=c),
        grid=(batch, nc),
        in_specs=[pl.BlockSpec(lb2.shape, lambda b, i: (0, 0)), col(3), col(4), col(5)],
        out_specs=[pl.BlockSpec((c, width), lambda b, i: (b * nc + i, 0)),
                   pl.BlockSpec((None, HEADS, HEAD_W, HEAD_W), lambda b, i: (b, 0, 0, 0))],
        out_shape=[jax.ShapeDtypeStruct((batch * seq, width), F32),
                   jax.ShapeDtypeStruct((batch, HEADS, HEAD_W, HEAD_W), F32)],
        scratch_shapes=[pltpu.VMEM((HEADS, HEAD_W, HEAD_W), F32)],
        compiler_params=_cparams(("parallel", "arbitrary")),
        name="hgrn_prompt",
    )(lb2, z, z, z)


def _hgrn_sample_kernel(lb_ref, q_ref, f_ref, i_ref, s_ref, o_ref, s_out_ref, *, rows):
    r_i = lax.broadcasted_iota(jnp.int32, (HEAD_W, HEAD_W), 0)
    c_i = lax.broadcasted_iota(jnp.int32, (HEAD_W, HEAD_W), 1)
    eye = r_i == c_i

    def to_col(x):
        return jnp.sum(jnp.where(eye, x, 0.0), axis=1, keepdims=True)

    for h in range(HEADS):
        sl = slice(h * HEAD_W, (h + 1) * HEAD_W)
        lb = _lower_bound(lb_ref, sl)
        f_all = lb + (1.0 - lb) * jax.nn.sigmoid(f_ref[:, sl])
        q_all = q_ref[:, sl]
        i_all = i_ref[:, sl]
        o_rows = []
        for r in range(rows):
            f_col = to_col(f_all[r:r + 1])
            q_col = to_col(q_all[r:r + 1])
            s_new = f_col * s_ref[r, h] + (1.0 - f_col) * i_all[r:r + 1]
            s_out_ref[r, h] = s_new
            o_rows.append(jnp.sum(q_col * s_new, axis=0, keepdims=True))
        o_ref[:, sl] = jnp.concatenate(o_rows, axis=0)


def _hgrn_sample(lb2, z, state, rows):
    nb = state.shape[0]
    width = HEADS * HEAD_W
    col = lambda off: pl.BlockSpec((rows, width), lambda i: (i, off))
    sspec = pl.BlockSpec((rows, HEADS, HEAD_W, HEAD_W), lambda i: (i, 0, 0, 0))
    return pl.pallas_call(
        functools.partial(_hgrn_sample_kernel, rows=rows),
        grid=(nb // rows,),
        in_specs=[pl.BlockSpec(lb2.shape, lambda i: (0, 0)), col(3), col(4), col(5), sspec],
        out_specs=[pl.BlockSpec((rows, width), lambda i: (i, 0)), sspec],
        out_shape=[jax.ShapeDtypeStruct((nb, width), F32),
                   jax.ShapeDtypeStruct(state.shape, F32)],
        compiler_params=_cparams(("parallel",)),
        name="hgrn_sample",
    )(lb2, z, z, z, state)


W_CHUNK = 512


def _head_rms(x, g):
    parts = []
    for h in range(HEADS):
        blk = x[:, h * HEAD_W:(h + 1) * HEAD_W]
        parts.append(_rms_rows(blk, g))
    return jnp.concatenate(parts, axis=1)


def _mixout_kernel(oa_ref, ob_ref, zg_ref, x_ref, oas_ref, obs_ref, zgs_ref, xs_ref, na_ref, nb_ref,
                   wa_hbm, wb_hbm, wo_hbm, o_ref, os_ref, wa_v, wb_v, wo_v, stage, sem):
    @pl.when(pl.program_id(0) == 0)
    def _():
        chunks = [(src, dst, r) for src, dst in ((wa_hbm, wa_v), (wb_hbm, wb_v), (wo_hbm, wo_v))
                  for r in range(0, src.shape[0], W_CHUNK)]

        def copy(c):
            src, _, r = chunks[c]
            return pltpu.make_async_copy(src.at[pl.ds(r, W_CHUNK), :], stage.at[c % 2], sem.at[c % 2])

        copy(0).start()
        for c in range(len(chunks)):
            if c + 1 < len(chunks):
                copy(c + 1).start()
            copy(c).wait()
            _, dst, r = chunks[c]
            dst[pl.ds(r, W_CHUNK), :] = stage[c % 2].astype(BF16)

    def mix(oa_r, ob_r, zg_r, x_r, out_r):
        w = oa_r.shape[1]
        n = out_r.shape[1]
        a = (_head_rms(oa_r[...], na_ref[...]) * (1.0 - LAM_INIT)).astype(BF16)
        og = zg_r[:, :w].astype(F32)
        b = (_head_rms(ob_r[...], nb_ref[...]) * (og * jax.nn.sigmoid(og))).astype(BF16)
        pa = jnp.dot(a, wa_v[...], preferred_element_type=F32)
        pb = jnp.dot(b, wb_v[...], preferred_element_type=F32)
        mg = (jax.nn.sigmoid(zg_r[:, w:w + n].astype(F32)) * pa
              + jax.nn.sigmoid(zg_r[:, w + n:].astype(F32)) * pb).astype(BF16)
        out_r[...] = x_r[...] + jnp.dot(mg, wo_v[...], preferred_element_type=F32)

    mix(oa_ref, ob_ref, zg_ref, x_ref, o_ref)

    @pl.when(pl.program_id(0) == pl.num_programs(0) - 1)
    def _():
        mix(oas_ref, obs_ref, zgs_ref, xs_ref, os_ref)


def _mixout(oa, ob, zg, x, oas, obs, zgs, xs, na, nb, wa, wb, wo, tm):
    m, w = oa.shape
    n = wo.shape[1]
    assert w % W_CHUNK == 0 and n % W_CHUNK == 0 and wa.shape[1] == n
    rows = lambda c: pl.BlockSpec((tm, c), lambda i: (i, 0))
    whole = lambda a: pl.BlockSpec(a.shape, lambda i: (0, 0))
    vec = pl.BlockSpec((1, HEAD_W), lambda i: (0, 0))
    anyspec = pl.BlockSpec(memory_space=pl.ANY)
    return pl.pallas_call(
        _mixout_kernel,
        grid=(m // tm,),
        in_specs=[rows(w), rows(w), rows(zg.shape[1]), rows(n),
                  whole(oas), whole(obs), whole(zgs), whole(xs),
                  vec, vec, anyspec, anyspec, anyspec],
        out_specs=[rows(n), whole(xs)],
        out_shape=[jax.ShapeDtypeStruct((m, n), F32), jax.ShapeDtypeStruct(xs.shape, F32)],
        scratch_shapes=[pltpu.VMEM((w, n), BF16), pltpu.VMEM((w, n), BF16), pltpu.VMEM((n, n), BF16),
                        pltpu.VMEM((2, W_CHUNK, n), F32), pltpu.SemaphoreType.DMA((2,))],
        compiler_params=_cparams(("arbitrary",)),
        name="mixout",
    )(oa, ob, zg, x, oas, obs, zgs, xs, na, nb, wa, wb, wo)


def _rope_tables(pos):
    half = HALF // 2
    inv = ROPE_THETA ** (-jnp.arange(half, dtype=F32) / half)
    ang = pos.astype(F32)[:, None] * inv[None, :]
    cos = jnp.tile(jnp.cos(ang), (1, LANES // half))
    sin = jnp.tile(jnp.sin(ang), (1, LANES // half))
    return cos, sin


def kernel(x_prompt, x_sample, cache_k, cache_v, state_hgrn, page_table, ffn1_norm, ffn1_w_gate, ffn1_w_up, ffn1_w_down, mix_norm, w_in, q_norm, k_norm, lambda_q1, lambda_k1, lambda_q2, lambda_k2, attn_sub_norm, hgrn_lower_bounds, hgrn_out_norm, w_proj_a, w_proj_b, w_out, ffn2_norm, ffn2_w_gate, ffn2_w_up, ffn2_w_down):
    batch, seq, d = x_prompt.shape
    nb = x_sample.shape[0]
    n_pages = page_table.shape[1]
    width = HEADS * HEAD_W
    n_a = 6 * width

    w1g, w1u, w1d = ffn1_w_gate[0], ffn1_w_up[0], ffn1_w_down[0]
    w2g, w2u, w2d = ffn2_w_gate[0], ffn2_w_up[0], ffn2_w_down[0]
    w_in_b = w_in[0]
    wa, wb, wo = w_proj_a[0], w_proj_b[0], w_out[0]
    lam = jnp.concatenate([lambda_q1, lambda_k1, lambda_q2, lambda_k2], axis=0).astype(F32)
    gq = jnp.tile(q_norm.astype(F32), (1, 2))
    gk = jnp.tile(k_norm.astype(F32), (1, 2))
    lb2 = hgrn_lower_bounds.astype(F32)

    xp = x_prompt.reshape(batch * seq, d)
    xs = x_sample.reshape(nb, d)

    x1p, x1s = _ffn(xp, xs, ffn1_norm, w1g, w1u, w1d, 1024, 256)
    zap, zas, zgp, zgs = _normmm(x1p, x1s, mix_norm, w_in_b, n_a, 2048, 512)
    cos_p, sin_p = _rope_tables(jnp.arange(seq))
    qb, kb, vb, kf, vf = _prep_prompt(zap, cos_p, sin_p, gq, gk, batch, seq, 256)
    pos_s = jnp.full((nb,), n_pages * PAGE, jnp.int32)
    cos_s, sin_s = _rope_tables(pos_s)
    qs, ks, vs = _prep_sample(zas, cos_s, sin_s, gq, gk)

    ck = cache_k.reshape(cache_k.shape[1], PAGE * HEADS, HEAD_W)
    cv = cache_v.reshape(cache_v.shape[1], PAGE * HEADS, HEAD_W)
    tiles = lambda a: a.reshape(nb, HEADS, HEAD_W)
    oa_p, oa_s = _attn(page_table, lam, qb, kb, vb, tiles(qs), tiles(ks), tiles(vs), ck, cv, 512)
    oa_s = oa_s.reshape(nb, width)
    ob_p, st_p = _hgrn_prompt(lb2, zap, batch, seq, 128)
    ob_s, st_s = _hgrn_sample(lb2, zas, state_hgrn[0], 8)

    x2p, x2s = _mixout(oa_p, ob_p, zgp, x1p, oa_s, ob_s, zgs, x1s,
                       attn_sub_norm, hgrn_out_norm, wa, wb, wo, 256)
    y_p, y_s = _ffn(x2p, x2s, ffn2_norm, w2g, w2u, w2d, 1024, 256)

    return (y_p.reshape(batch, seq, d),
            y_s.reshape(nb, 1, d),
            kf.reshape(1, batch, seq, HEADS, HEAD_W),
            vf.reshape(1, batch, seq, HEADS, HEAD_W),
            ks.reshape(1, nb, 1, HEADS, HEAD_W),
            vs.reshape(1, nb, 1, HEADS, HEAD_W),
            st_p.reshape(1, batch, HEADS, HEAD_W, HEAD_W),
            st_s.reshape(1, nb, HEADS, HEAD_W, HEAD_W))
```

```python
import functools
import math

import jax
import jax.numpy as jnp
from jax import lax
from jax.experimental import pallas as pl
from jax.experimental.pallas import tpu as pltpu

F32 = jnp.float32
BF16 = jnp.bfloat16

EPS = 1e-6
ROPE_THETA = 10000.0
HEADS = 8
HEAD_W = 128
HALF = 64
PAGE = 128
LAM_INIT = 0.8 - 0.6 * math.exp(-0.3 * 0)
LANES = 128
VMEM_LIMIT = 56 * 1024 * 1024
NEG = -1e30

FFN_ROWS = 1024
FFN_HIDDEN = 256
INPROJ_ROWS = 2048
INPROJ_COLS = 512
PREP_ROWS = 256
ATTN_BLOCK = 512
HGRN_CHUNK = 128
HGRN_SAMPLE_ROWS = 16
MIX_ROWS = 256


def _cparams(sem):
    return pltpu.CompilerParams(dimension_semantics=sem, vmem_limit_bytes=VMEM_LIMIT)


def _nt_dot(a, b):
    return lax.dot_general(a, b, (((1,), (1,)), ((), ())), preferred_element_type=F32)


def _rms_rows(x, g):
    ms = jnp.mean(x * x, axis=-1, keepdims=True)
    return x * lax.rsqrt(ms + EPS) * g


def _ffn_kernel(xp_ref, xs_ref, g_ref, wg_ref, wu_ref, wd_ref, op_ref, os_ref, h_ref):
    j = pl.program_id(1)
    tp = xp_ref.shape[0]

    @pl.when(j == 0)
    def _():
        xp = xp_ref[...]
        xs = xs_ref[...]
        g = g_ref[...]
        h_ref[:tp] = _rms_rows(xp, g).astype(BF16)
        h_ref[tp:] = _rms_rows(xs, g).astype(BF16)
        op_ref[...] = xp
        os_ref[...] = xs

    h = h_ref[...]
    a = jnp.dot(h, wg_ref[...].astype(BF16), preferred_element_type=F32)
    u = jnp.dot(h, wu_ref[...].astype(BF16), preferred_element_type=F32)
    t = (a * jax.nn.sigmoid(a) * (0.5 * u)).astype(BF16)
    y = jnp.dot(t, wd_ref[...].astype(BF16), preferred_element_type=F32)
    op_ref[...] += y[:tp]
    os_ref[...] += y[tp:]


def _ffn(xp, xs, g, wg, wu, wd, tm, tf):
    m, d = xp.shape
    f = wg.shape[1]
    nt = m // tm
    ts = xs.shape[0] // nt
    rows = lambda t: pl.BlockSpec((t, d), lambda i, j: (i, 0))
    acc = lambda t: pl.BlockSpec((t, d), lambda i, j: (i, 0), pipeline_mode=pl.Buffered(1))
    return pl.pallas_call(
        _ffn_kernel,
        grid=(nt, f // tf),
        in_specs=[
            rows(tm), rows(ts),
            pl.BlockSpec((1, d), lambda i, j: (0, 0)),
            pl.BlockSpec((d, tf), lambda i, j: (0, j)),
            pl.BlockSpec((d, tf), lambda i, j: (0, j)),
            pl.BlockSpec((tf, d), lambda i, j: (j, 0)),
        ],
        out_specs=[acc(tm), acc(ts)],
        out_shape=[jax.ShapeDtypeStruct(xp.shape, F32), jax.ShapeDtypeStruct(xs.shape, F32)],
        scratch_shapes=[pltpu.VMEM((tm + ts, d), BF16)],
        compiler_params=_cparams(("parallel", "arbitrary")),
        name="ffn",
    )(xp, xs, g, wg, wu, wd)


def _normmm_kernel(xp_ref, xs_ref, g_ref, w_ref, ap_ref, as_ref, gp_ref, gs_ref, h_ref, *, a_tiles):
    j = pl.program_id(1)
    tp = xp_ref.shape[0]

    @pl.when(j == 0)
    def _():
        g = g_ref[...]
        h_ref[:tp] = _rms_rows(xp_ref[...], g).astype(BF16)
        h_ref[tp:] = _rms_rows(xs_ref[...], g).astype(BF16)

    z = jnp.dot(h_ref[...], w_ref[...].astype(BF16), preferred_element_type=F32)

    @pl.when(j < a_tiles)
    def _():
        ap_ref[...] = z[:tp]
        as_ref[...] = z[tp:]

    @pl.when(j >= a_tiles)
    def _():
        gp_ref[...] = z[:tp].astype(BF16)
        gs_ref[...] = z[tp:].astype(BF16)


def _normmm(xp, xs, g, w, n_a, tm, tn):
    m, d = xp.shape
    n = w.shape[1]
    nt = m // tm
    ts = xs.shape[0] // nt
    a_tiles = n_a // tn
    rows = lambda t: pl.BlockSpec((t, d), lambda i, j: (i, 0), pipeline_mode=pl.Buffered(1))
    a_out = lambda t: pl.BlockSpec((t, tn), lambda i, j: (i, jnp.minimum(j, a_tiles - 1)))
    g_out = lambda t: pl.BlockSpec((t, tn), lambda i, j: (i, jnp.maximum(j - a_tiles, 0)))
    return pl.pallas_call(
        functools.partial(_normmm_kernel, a_tiles=a_tiles),
        grid=(nt, n // tn),
        in_specs=[
            rows(tm), rows(ts),
            pl.BlockSpec((1, d), lambda i, j: (0, 0)),
            pl.BlockSpec((d, tn), lambda i, j: (0, j)),
        ],
        out_specs=[a_out(tm), a_out(ts), g_out(tm), g_out(ts)],
        out_shape=[jax.ShapeDtypeStruct((m, n_a), F32), jax.ShapeDtypeStruct((xs.shape[0], n_a), F32),
                   jax.ShapeDtypeStruct((m, n - n_a), BF16), jax.ShapeDtypeStruct((xs.shape[0], n - n_a), BF16)],
        scratch_shapes=[pltpu.VMEM((tm + ts, d), BF16)],
        compiler_params=_cparams(("parallel", "arbitrary")),
        name="normmm",
    )(xp, xs, g, w)


def _group_ms(x):
    r = lax.broadcasted_iota(jnp.int32, (LANES, LANES), 0)
    c = lax.broadcasted_iota(jnp.int32, (LANES, LANES), 1)
    grp = jnp.where((r >> 6) == (c >> 6), 1.0, 0.0).astype(BF16)
    x2 = x * x
    hi = x2.astype(BF16)
    r1 = x2 - hi.astype(F32)
    mid = r1.astype(BF16)
    lo = (r1 - mid.astype(F32)).astype(BF16)
    s = (jnp.dot(hi, grp, preferred_element_type=F32)
         + jnp.dot(mid, grp, preferred_element_type=F32)
         + jnp.dot(lo, grp, preferred_element_type=F32))
    return s * (1.0 / HALF)


def _norm_rope(x, g, cos, sin):
    y = x * lax.rsqrt(_group_ms(x) + EPS) * g
    lane = lax.broadcasted_iota(jnp.int32, y.shape, 1)
    first = (lane & (HALF - 1)) < (HALF // 2)
    rot = jnp.where(first, -pltpu.roll(y, LANES - HALF // 2, 1), pltpu.roll(y, HALF // 2, 1))
    return y * cos + rot * sin


def _prep_prompt_kernel(q_ref, k_ref, v_ref, cos_ref, sin_ref, gq_ref, gk_ref,
                        qb_ref, kb_ref, vb_ref, kf_ref, vf_ref):
    cos = cos_ref[...]
    sin = sin_ref[...]
    gq = gq_ref[...]
    gk = gk_ref[...]
    vf_ref[...] = v_ref[...]
    for h in range(HEADS):
        sl = slice(h * HEAD_W, (h + 1) * HEAD_W)
        k = _norm_rope(k_ref[:, sl], gk, cos, sin)
        kf_ref[:, sl] = k
        kb_ref[h] = k.astype(BF16)
        qb_ref[h] = (_norm_rope(q_ref[:, sl], gq, cos, sin) * (HALF ** -0.5)).astype(BF16)
        vb_ref[h] = v_ref[:, sl].astype(BF16)


def _prep_prompt(z, cos, sin, gq, gk, batch, seq, tm):
    nt = seq // tm
    width = HEADS * HEAD_W
    row = lambda b, i: b * nt + i
    hm = pl.BlockSpec((None, HEADS, tm, HEAD_W), lambda b, i: (b, 0, i, 0))
    flat = pl.BlockSpec((tm, width), lambda b, i: (row(b, i), 0))
    vec = pl.BlockSpec((1, HEAD_W), lambda b, i: (0, 0))
    tab = pl.BlockSpec((tm, HEAD_W), lambda b, i: (i, 0))
    hm_shape = jax.ShapeDtypeStruct((batch, HEADS, seq, HEAD_W), BF16)
    flat_shape = jax.ShapeDtypeStruct((batch * seq, width), F32)
    return pl.pallas_call(
        _prep_prompt_kernel,
        grid=(batch, nt),
        in_specs=[
            pl.BlockSpec((tm, width), lambda b, i: (row(b, i), 0)),
            pl.BlockSpec((tm, width), lambda b, i: (row(b, i), 1)),
            pl.BlockSpec((tm, width), lambda b, i: (row(b, i), 2)),
            tab, tab, vec, vec,
        ],
        out_specs=[hm, hm, hm, flat, flat],
        out_shape=[hm_shape, hm_shape, hm_shape, flat_shape, flat_shape],
        compiler_params=_cparams(("parallel", "parallel")),
        name="prep_prompt",
    )(z, z, z, cos, sin, gq, gk)


def _prep_sample_kernel(q_ref, k_ref, v_ref, cos_ref, sin_ref, gq_ref, gk_ref,
                        qf_ref, kf_ref, vf_ref):
    cos = cos_ref[...]
    sin = sin_ref[...]
    qf_ref[...] = _norm_rope(q_ref[...], gq_ref[...], cos, sin) * (HALF ** -0.5)
    kf_ref[...] = _norm_rope(k_ref[...], gk_ref[...], cos, sin)
    vf_ref[...] = v_ref[...]


def _prep_sample(z, cos, sin, gq, gk):
    m = z.shape[0]
    blk = lambda off: pl.BlockSpec((m, HEAD_W), lambda h: (0, off + h))
    vec = pl.BlockSpec((1, HEAD_W), lambda h: (0, 0))
    tab = pl.BlockSpec((m, HEAD_W), lambda h: (0, 0))
    shape = jax.ShapeDtypeStruct((m, HEADS * HEAD_W), F32)
    return pl.pallas_call(
        _prep_sample_kernel,
        grid=(HEADS,),
        in_specs=[blk(0), blk(HEADS), blk(2 * HEADS), tab, tab, vec, vec],
        out_specs=[blk(0), blk(0), blk(0)],
        out_shape=[shape, shape, shape],
        compiler_params=_cparams(("parallel",)),
        name="prep_sample",
    )(z, z, z, cos, sin, gq, gk)


def _lambda(lam_ref):
    l = lam_ref[...]
    s1 = jnp.sum(l[0:1] * l[1:2], axis=1, keepdims=True)
    s2 = jnp.sum(l[2:3] * l[3:4], axis=1, keepdims=True)
    return jnp.exp(s1) - jnp.exp(s2) + LAM_INIT


def _prompt_block(qi, lam, q_ref, k_ref, v_ref, o_ref, m_ref, l_ref, acc_ref, tq):
    q = q_ref[...]
    lane = lax.broadcasted_iota(jnp.int32, q.shape, 1)
    zero = jnp.zeros_like(q)
    qs = jnp.concatenate([jnp.where(lane < HALF, q, zero),
                          jnp.where(lane >= HALF, q, zero)], axis=0)

    m_ref[...] = jnp.full(m_ref.shape, NEG, F32)
    l_ref[...] = jnp.zeros(l_ref.shape, F32)
    acc_ref[...] = jnp.zeros(acc_ref.shape, F32)

    def step(kb, masked):
        start = pl.multiple_of(kb * tq, tq)
        k = k_ref[pl.ds(start, tq), :]
        v = v_ref[pl.ds(start, tq), :]
        s = _nt_dot(qs, k)
        if masked:
            r = lax.broadcasted_iota(jnp.int32, s.shape, 0)
            c = lax.broadcasted_iota(jnp.int32, s.shape, 1)
            r = jnp.where(r >= tq, r - tq, r)
            s = jnp.where(r >= c, s, NEG)
        m_prev = m_ref[...]
        m_new = jnp.maximum(m_prev, jnp.max(s, axis=1, keepdims=True))
        alpha = jnp.exp(m_prev - m_new)
        p = jnp.exp(s - jnp.concatenate([m_new] * (tq // LANES), axis=1))
        lsum = p[:, 0:LANES]
        for cblk in range(1, tq // LANES):
            lsum = lsum + p[:, cblk * LANES:(cblk + 1) * LANES]
        l_ref[...] = alpha * l_ref[...] + lsum
        acc_ref[...] = alpha * acc_ref[...] + jnp.dot(p.astype(BF16), v, preferred_element_type=F32)
        m_ref[...] = m_new

    def body(pair, carry):
        step(2 * pair, False)
        step(2 * pair + 1, False)
        return carry

    lax.fori_loop(0, qi // 2, body, 0)

    @pl.when(qi % 2 == 1)
    def _():
        step(qi - 1, False)

    step(qi, True)

    l = jnp.sum(l_ref[...], axis=1, keepdims=True)
    o = acc_ref[...] / l
    o_ref[...] = o[:tq] - lam * o[tq:]


def _sample_seq(lam, q, kn, vn, kpast, vpast):
    q2 = jnp.concatenate([q, q], axis=0)
    row = lax.broadcasted_iota(jnp.int32, q2.shape, 0)
    lane = lax.broadcasted_iota(jnp.int32, q2.shape, 1)
    qm = jnp.where((lane >> 6) == (row >> 3), q2, 0.0)

    s = _nt_dot(qm.astype(BF16), kpast.astype(BF16))
    own = ((lax.broadcasted_iota(jnp.int32, s.shape, 1) & (HEADS - 1))
           == (lax.broadcasted_iota(jnp.int32, s.shape, 0) & (HEADS - 1)))
    s = jnp.where(own, s, NEG)
    s_new = jnp.sum(qm * jnp.concatenate([kn, kn], axis=0), axis=1, keepdims=True)
    m = jnp.maximum(jnp.max(s, axis=1, keepdims=True), s_new)
    p = jnp.exp(s - m)
    p_new = jnp.exp(s_new - m)
    inv = 1.0 / (jnp.sum(p, axis=1, keepdims=True) + p_new)
    c0 = inv[:HEADS]
    c1 = lam * inv[HEADS:]
    pd = p[:HEADS] * c0 - p[HEADS:] * c1
    pd_new = p_new[:HEADS] * c0 - p_new[HEADS:] * c1
    return jnp.dot(pd.astype(BF16), vpast.astype(BF16), preferred_element_type=F32) + pd_new * vn


def _causal_order(step, n):
    return jnp.where(step % 2 == 0, n - 1 - step // 2, step // 2)


def _attn_kernel(pt_ref, lam_ref, q_ref, k_ref, v_ref, qs_ref, kn_ref, vn_ref, ck_hbm, cv_hbm,
                 o_ref, os_ref, m_ref, l_ref, acc_ref, kbuf, vbuf, sem, *, tq, n_pages):
    step = pl.program_id(2)
    qi = _causal_order(step, pl.num_programs(2))
    seq = (pl.program_id(0) * pl.num_programs(1) + pl.program_id(1)) * pl.num_programs(2) + step
    n_seq = pl.num_programs(0) * pl.num_programs(1) * pl.num_programs(2)
    slot = seq % 2
    page_rows = PAGE * HEADS

    def page_copies(seq_, slot_):
        cps = []
        for p in range(n_pages):
            pg = pt_ref[seq_, p]
            dst = pl.ds(p * page_rows, page_rows)
            cps.append(pltpu.make_async_copy(ck_hbm.at[pg], kbuf.at[slot_, dst, :], sem.at[0, slot_]))
            cps.append(pltpu.make_async_copy(cv_hbm.at[pg], vbuf.at[slot_, dst, :], sem.at[1, slot_]))
        return cps

    @pl.when(seq == 0)
    def _():
        for cp in page_copies(0, 0):
            cp.start()

    @pl.when(seq + 1 < n_seq)
    def _():
        for cp in page_copies(seq + 1, 1 - slot):
            cp.start()

    lam = _lambda(lam_ref)
    _prompt_block(qi, lam, q_ref, k_ref, v_ref, o_ref, m_ref, l_ref, acc_ref, tq)

    for cp in page_copies(seq, slot):
        cp.wait()
    os_ref[...] = _sample_seq(lam, qs_ref[...], kn_ref[...], vn_ref[...], kbuf[slot], vbuf[slot])


def _attn(page_table, lam, qb, kb, vb, qs, kn, vn, ck, cv, tq):
    batch, heads, seq, w = qb.shape
    nq = seq // tq
    nb, n_pages = page_table.shape
    assert batch * heads * nq == nb, "one sample sequence per prompt query block"
    rows = n_pages * PAGE * HEADS
    sidx = lambda b, h, i, pt: ((b * heads + h) * nq + i, 0, 0)
    tile = pl.BlockSpec((None, HEADS, HEAD_W), sidx)
    kv = pl.BlockSpec((None, None, seq, w), lambda b, h, i, pt: (b, h, 0, 0))
    grid_spec = pltpu.PrefetchScalarGridSpec(
        num_scalar_prefetch=1,
        grid=(batch, heads, nq),
        in_specs=[
            pl.BlockSpec(lam.shape, lambda b, h, i, pt: (0, 0)),
            pl.BlockSpec((None, None, tq, w), lambda b, h, i, pt: (b, h, _causal_order(i, nq), 0)),
            kv, kv,
            tile, tile, tile,
            pl.BlockSpec(memory_space=pl.ANY),
            pl.BlockSpec(memory_space=pl.ANY),
        ],
        out_specs=[pl.BlockSpec((tq, w), lambda b, h, i, pt: (b * nq + _causal_order(i, nq), h)), tile],
        scratch_shapes=[pltpu.VMEM((2 * tq, LANES), F32),
                        pltpu.VMEM((2 * tq, LANES), F32),
                        pltpu.VMEM((2 * tq, w), F32),
                        pltpu.VMEM((2, rows, HEAD_W), F32),
                        pltpu.VMEM((2, rows, HEAD_W), F32),
                        pltpu.SemaphoreType.DMA((2, 2))],
    )
    return pl.pallas_call(
        functools.partial(_attn_kernel, tq=tq, n_pages=n_pages),
        grid_spec=grid_spec,
        out_shape=[jax.ShapeDtypeStruct((batch * seq, heads * w), F32),
                   jax.ShapeDtypeStruct((nb, HEADS, HEAD_W), F32)],
        compiler_params=_cparams(("arbitrary", "arbitrary", "arbitrary")),
        name="attn",
    )(page_table, lam, qb, kb, vb, qs, kn, vn, ck, cv)


def _lower_bound(lb_ref, sl):
    a = lb_ref[:, sl]
    mx = jnp.maximum(a[0:1], a[1:2])
    e0 = jnp.exp(a[0:1] - mx)
    e1 = jnp.exp(a[1:2] - mx)
    return e0 / (e0 + e1)


def _split3(x):
    hi = x.astype(BF16)
    r1 = x - hi.astype(F32)
    mid = r1.astype(BF16)
    lo = (r1 - mid.astype(F32)).astype(BF16)
    return hi, mid, lo


def _pair_ref(b, m, c):
    parts = [jnp.broadcast_to(b[2 * m * p + m - 1:2 * m * p + m, :], (2 * m, b.shape[1]))
             for p in range(c // (2 * m))]
    return parts[0] if len(parts) == 1 else jnp.concatenate(parts, axis=0)


def _hgrn_prompt_kernel(lb_ref, q_ref, f_ref, i_ref, o_ref, s_out_ref, st_ref, *, c):
    ci = pl.program_id(1)

    @pl.when(ci == 0)
    def _():
        st_ref[...] = jnp.zeros(st_ref.shape, F32)

    row = lax.broadcasted_iota(jnp.int32, (c, c), 0)
    col = lax.broadcasted_iota(jnp.int32, (c, c), 1)
    tri = jnp.where(col <= row, 1.0, 0.0).astype(BF16)
    same_sub = (row >> 4) == (col >> 4)
    causal = col <= row
    level_masks = {16: (row >> 5) == (col >> 5), 32: (row >> 6) == (col >> 6), 64: None}
    width = HEADS * HEAD_W
    trow = lax.broadcasted_iota(jnp.int32, (c, width), 0)
    sub = 16

    lb = _lower_bound(lb_ref, slice(None))
    q = q_ref[...]
    v = i_ref[...]
    f = lb + (1.0 - lb) * jax.nn.sigmoid(f_ref[...])
    k = 1.0 - f
    hi, mid, lo = _split3(jnp.log(f))
    b = (jnp.dot(tri, hi, preferred_element_type=F32)
         + jnp.dot(tri, mid, preferred_element_type=F32)
         + jnp.dot(tri, lo, preferred_element_type=F32))

    ref_d = jnp.concatenate(
        [jnp.zeros((sub, width), F32)]
        + [jnp.broadcast_to(b[sub * j - 1:sub * j, :], (sub, width)) for j in range(1, c // sub)],
        axis=0)
    arg_d = b - ref_d
    q_d = (q * jnp.exp(arg_d)).astype(BF16)
    k_d = (k * jnp.exp(-arg_d)).astype(BF16)
    levels = []
    for m in (16, 32, 64):
        odd = ((trow >> int(math.log2(m))) & 1) == 1
        d = b - _pair_ref(b, m, c)
        e = jnp.exp(jnp.where(odd, d, -d))
        levels.append((level_masks[m],
                       jnp.where(odd, q * e, 0.0).astype(BF16),
                       jnp.where(odd, 0.0, k * e).astype(BF16)))
    b_last = b[c - 1:c, :]
    q_t = (q * jnp.exp(b)).astype(BF16)
    k_t = (k * jnp.exp(b_last - b)).astype(BF16)
    decay = jnp.exp(b_last)
    v_b = v.astype(BF16)

    for h in range(HEADS):
        sl = slice(h * HEAD_W, (h + 1) * HEAD_W)
        att = jnp.where(same_sub, jnp.where(causal, _nt_dot(q_d[:, sl], k_d[:, sl]), 0.0), 0.0)
        for msk, q_m, k_m in levels:
            p_m = _nt_dot(q_m[:, sl], k_m[:, sl])
            att = att + (p_m if msk is None else jnp.where(msk, p_m, 0.0))
        st = st_ref[h]
        o_ref[:, sl] = (jnp.dot(att.astype(BF16), v_b[:, sl], preferred_element_type=F32)
                        + _nt_dot(q_t[:, sl], st.astype(BF16)))
        st_ref[h] = st * decay[:, sl] + jnp.dot(v[:, sl].T.astype(BF16), k_t[:, sl],
                                                preferred_element_type=F32)

    @pl.when(ci == pl.num_programs(1) - 1)
    def _():
        for h in range(HEADS):
            s_out_ref[h] = st_ref[h].T


def _hgrn_prompt(lb2, z, batch, seq, c):
    nc = seq // c
    width = HEADS * HEAD_W
    col = lambda off: pl.BlockSpec((c, width), lambda b, i: (b * nc + i, off))
    return pl.pallas_call(
        functools.partial(_hgrn_prompt_kernel, c=c),
        grid=(batch, nc),
        in_specs=[pl.BlockSpec(lb2.shape, lambda b, i: (0, 0)), col(3), col(4), col(5)],
        out_specs=[pl.BlockSpec((c, width), lambda b, i: (b * nc + i, 0)),
                   pl.BlockSpec((None, HEADS, HEAD_W, HEAD_W), lambda b, i: (b, 0, 0, 0))],
        out_shape=[jax.ShapeDtypeStruct((batch * seq, width), F32),
                   jax.ShapeDtypeStruct((batch, HEADS, HEAD_W, HEAD_W), F32)],
        scratch_shapes=[pltpu.VMEM((HEADS, HEAD_W, HEAD_W), F32)],
        compiler_params=_cparams(("parallel", "arbitrary")),
        name="hgrn_prompt",
    )(lb2, z, z, z)


def _hgrn_sample_kernel(lb_ref, q_ref, f_ref, i_ref, s_ref, o_ref, s_out_ref, *, rows):
    r_i = lax.broadcasted_iota(jnp.int32, (HEAD_W, HEAD_W), 0)
    c_i = lax.broadcasted_iota(jnp.int32, (HEAD_W, HEAD_W), 1)
    eye = r_i == c_i

    def to_col(x):
        return jnp.sum(jnp.where(eye, x, 0.0), axis=1, keepdims=True)

    for h in range(HEADS):
        sl = slice(h * HEAD_W, (h + 1) * HEAD_W)
        lb = _lower_bound(lb_ref, sl)
        f_all = lb + (1.0 - lb) * jax.nn.sigmoid(f_ref[:, sl])
        q_all = q_ref[:, sl]
        i_all = i_ref[:, sl]
        o_rows = []
        for r in range(rows):
            f_col = to_col(f_all[r:r + 1])
            q_col = to_col(q_all[r:r + 1])
            s_new = f_col * s_ref[r, h] + (1.0 - f_col) * i_all[r:r + 1]
            s_out_ref[r, h] = s_new
            o_rows.append(jnp.sum(q_col * s_new, axis=0, keepdims=True))
        o_ref[:, sl] = jnp.concatenate(o_rows, axis=0)


def _hgrn_sample(lb2, z, state, rows):
    nb = state.shape[0]
    width = HEADS * HEAD_W
    col = lambda off: pl.BlockSpec((rows, width), lambda i: (i, off))
    sspec = pl.BlockSpec((rows, HEADS, HEAD_W, HEAD_W), lambda i: (i, 0, 0, 0))
    return pl.pallas_call(
        functools.partial(_hgrn_sample_kernel, rows=rows),
        grid=(nb // rows,),
        in_specs=[pl.BlockSpec(lb2.shape, lambda i: (0, 0)), col(3), col(4), col(5), sspec],
        out_specs=[pl.BlockSpec((rows, width), lambda i: (i, 0)), sspec],
        out_shape=[jax.ShapeDtypeStruct((nb, width), F32),
                   jax.ShapeDtypeStruct(state.shape, F32)],
        compiler_params=_cparams(("parallel",)),
        name="hgrn_sample",
    )(lb2, z, z, z, state)


W_CHUNK = 512


def _head_rms(x, g):
    parts = []
    for h in range(HEADS):
        blk = x[:, h * HEAD_W:(h + 1) * HEAD_W]
        parts.append(_rms_rows(blk, g))
    return jnp.concatenate(parts, axis=1)


def _mixout_kernel(oa_ref, ob_ref, zg_ref, x_ref, oas_ref, obs_ref, zgs_ref, xs_ref, na_ref, nb_ref,
                   wa_hbm, wb_hbm, wo_hbm, o_ref, os_ref, wa_v, wb_v, wo_v, stage, sem):
    @pl.when(pl.program_id(0) == 0)
    def _():
        chunks = [(src, dst, r) for src, dst in ((wa_hbm, wa_v), (wb_hbm, wb_v), (wo_hbm, wo_v))
                  for r in range(0, src.shape[0], W_CHUNK)]

        def copy(c):
            src, _, r = chunks[c]
            return pltpu.make_async_copy(src.at[pl.ds(r, W_CHUNK), :], stage.at[c % 2], sem.at[c % 2])

        copy(0).start()
        for c in range(len(chunks)):
            if c + 1 < len(chunks):
                copy(c + 1).start()
            copy(c).wait()
            _, dst, r = chunks[c]
            dst[pl.ds(r, W_CHUNK), :] = stage[c % 2].astype(BF16)

    def mix(oa_r, ob_r, zg_r, x_r, out_r):
        w = oa_r.shape[1]
        n = out_r.shape[1]
        a = (_head_rms(oa_r[...], na_ref[...]) * (1.0 - LAM_INIT)).astype(BF16)
        og = zg_r[:, :w].astype(F32)
        b = (_head_rms(ob_r[...], nb_ref[...]) * (og * jax.nn.sigmoid(og))).astype(BF16)
        pa = jnp.dot(a, wa_v[...], preferred_element_type=F32)
        pb = jnp.dot(b, wb_v[...], preferred_element_type=F32)
        mg = (jax.nn.sigmoid(zg_r[:, w:w + n].astype(F32)) * pa
              + jax.nn.sigmoid(zg_r[:, w + n:].astype(F32)) * pb).astype(BF16)
        out_r[...] = x_r[...] + jnp.dot(mg, wo_v[...], preferred_element_type=F32)

    mix(oa_ref, ob_ref, zg_ref, x_ref, o_ref)

    @pl.when(pl.program_id(0) == pl.num_programs(0) - 1)
    def _():
        mix(oas_ref, obs_ref, zgs_ref, xs_ref, os_ref)


def _mixout(oa, ob, zg, x, oas, obs, zgs, xs, na, nb, wa, wb, wo, tm):
    m, w = oa.shape
    n = wo.shape[1]
    assert w % W_CHUNK == 0 and n % W_CHUNK == 0 and wa.shape[1] == n
    rows = lambda c: pl.BlockSpec((tm, c), lambda i: (i, 0))
    whole = lambda a: pl.BlockSpec(a.shape, lambda i: (0, 0))
    vec = pl.BlockSpec((1, HEAD_W), lambda i: (0, 0))
    anyspec = pl.BlockSpec(memory_space=pl.ANY)
    return pl.pallas_call(
        _mixout_kernel,
        grid=(m // tm,),
        in_specs=[rows(w), rows(w), rows(zg.shape[1]), rows(n),
                  whole(oas), whole(obs), whole(zgs), whole(xs),
                  vec, vec, anyspec, anyspec, anyspec],
        out_specs=[rows(n), whole(xs)],
        out_shape=[jax.ShapeDtypeStruct((m, n), F32), jax.ShapeDtypeStruct(xs.shape, F32)],
        scratch_shapes=[pltpu.VMEM((w, n), BF16), pltpu.VMEM((w, n), BF16), pltpu.VMEM((n, n), BF16),
                        pltpu.VMEM((2, W_CHUNK, n), F32), pltpu.SemaphoreType.DMA((2,))],
        compiler_params=_cparams(("arbitrary",)),
        name="mixout",
    )(oa, ob, zg, x, oas, obs, zgs, xs, na, nb, wa, wb, wo)


def _rope_tables(pos):
    half = HALF // 2
    inv = ROPE_THETA ** (-jnp.arange(half, dtype=F32) / half)
    ang = pos.astype(F32)[:, None] * inv[None, :]
    cos = jnp.tile(jnp.cos(ang), (1, LANES // half))
    sin = jnp.tile(jnp.sin(ang), (1, LANES // half))
    return cos, sin


def kernel(x_prompt, x_sample, cache_k, cache_v, state_hgrn, page_table, ffn1_norm, ffn1_w_gate, ffn1_w_up, ffn1_w_down, mix_norm, w_in, q_norm, k_norm, lambda_q1, lambda_k1, lambda_q2, lambda_k2, attn_sub_norm, hgrn_lower_bounds, hgrn_out_norm, w_proj_a, w_proj_b, w_out, ffn2_norm, ffn2_w_gate, ffn2_w_up, ffn2_w_down):
    batch, seq, d = x_prompt.shape
    nb = x_sample.shape[0]
    n_pages = page_table.shape[1]
    width = HEADS * HEAD_W
    n_a = 6 * width

    w1g, w1u, w1d = ffn1_w_gate[0], ffn1_w_up[0], ffn1_w_down[0]
    w2g, w2u, w2d = ffn2_w_gate[0], ffn2_w_up[0], ffn2_w_down[0]
    w_in_b = w_in[0]
    wa, wb, wo = w_proj_a[0], w_proj_b[0], w_out[0]
    lam = jnp.concatenate([lambda_q1, lambda_k1, lambda_q2, lambda_k2], axis=0).astype(F32)
    gq = jnp.tile(q_norm.astype(F32), (1, 2))
    gk = jnp.tile(k_norm.astype(F32), (1, 2))
    lb2 = hgrn_lower_bounds.astype(F32)

    xp = x_prompt.reshape(batch * seq, d)
    xs = x_sample.reshape(nb, d)

    x1p, x1s = _ffn(xp, xs, ffn1_norm, w1g, w1u, w1d, FFN_ROWS, FFN_HIDDEN)
    zap, zas, zgp, zgs = _normmm(x1p, x1s, mix_norm, w_in_b, n_a, INPROJ_ROWS, INPROJ_COLS)
    cos_p, sin_p = _rope_tables(jnp.arange(seq))
    qb, kb, vb, kf, vf = _prep_prompt(zap, cos_p, sin_p, gq, gk, batch, seq, PREP_ROWS)
    pos_s = jnp.full((nb,), n_pages * PAGE, jnp.int32)
    cos_s, sin_s = _rope_tables(pos_s)
    qs, ks, vs = _prep_sample(zas, cos_s, sin_s, gq, gk)

    ck = cache_k.reshape(cache_k.shape[1], PAGE * HEADS, HEAD_W)
    cv = cache_v.reshape(cache_v.shape[1], PAGE * HEADS, HEAD_W)
    tiles = lambda a: a.reshape(nb, HEADS, HEAD_W)
    oa_p, oa_s = _attn(page_table, lam, qb, kb, vb, tiles(qs), tiles(ks), tiles(vs), ck, cv, ATTN_BLOCK)
    oa_s = oa_s.reshape(nb, width)
    ob_p, st_p = _hgrn_prompt(lb2, zap, batch, seq, HGRN_CHUNK)
    ob_s, st_s = _hgrn_sample(lb2, zas, state_hgrn[0], HGRN_SAMPLE_ROWS)

    x2p, x2s = _mixout(oa_p, ob_p, zgp, x1p, oa_s, ob_s, zgs, x1s,
                       attn_sub_norm, hgrn_out_norm, wa, wb, wo, MIX_ROWS)
    y_p, y_s = _ffn(x2p, x2s, ffn2_norm, w2g, w2u, w2d, FFN_ROWS, FFN_HIDDEN)

    return (y_p.reshape(batch, seq, d),
            y_s.reshape(nb, 1, d),
            kf.reshape(1, batch, seq, HEADS, HEAD_W),
            vf.reshape(1, batch, seq, HEADS, HEAD_W),
            ks.reshape(1, nb, 1, HEADS, HEAD_W),
            vs.reshape(1, nb, 1, HEADS, HEAD_W),
            st_p.reshape(1, batch, HEADS, HEAD_W, HEAD_W),
            st_s.reshape(1, nb, HEADS, HEAD_W, HEAD_W))
```

```python
import functools
import math

import jax
import jax.numpy as jnp
from jax import lax
from jax.experimental import pallas as pl
from jax.experimental.pallas import tpu as pltpu

F32 = jnp.float32
BF16 = jnp.bfloat16

EPS = 1e-6
ROPE_THETA = 10000.0
HEADS = 8
HEAD_W = 128
HALF = 64
PAGE = 128
LAM_INIT = 0.8 - 0.6 * math.exp(-0.3 * 0)
LANES = 128
VMEM_LIMIT = 56 * 1024 * 1024
NEG = -1e30

FFN_ROWS = 1024
FFN_HIDDEN = 256
FFN_WEIGHT_SLOTS = 3
INPROJ_ROWS = 2048
INPROJ_COLS = 512
PREP_ROWS = 256
ATTN_BLOCK = 512
HGRN_CHUNK = 128
HGRN_SAMPLE_ROWS = 16
MIX_ROWS = 256


def _cparams(sem):
    return pltpu.CompilerParams(dimension_semantics=sem, vmem_limit_bytes=VMEM_LIMIT)


def _nt_dot(a, b):
    return lax.dot_general(a, b, (((1,), (1,)), ((), ())), preferred_element_type=F32)


def _rms_rows(x, g):
    ms = jnp.mean(x * x, axis=-1, keepdims=True)
    return x * lax.rsqrt(ms + EPS) * g


def _ffn_kernel(xp_ref, xs_ref, g_ref, wg_hbm, wu_hbm, wd_hbm, op_ref, os_ref, h_ref,
                wg_buf, wu_buf, wd_buf, sem, *, tf):
    j = pl.program_id(1)
    nj = pl.num_programs(1)
    step = pl.program_id(0) * nj + j
    n_steps = pl.num_programs(0) * nj
    tp = xp_ref.shape[0]
    ahead = FFN_WEIGHT_SLOTS - 1

    def tile_copies(s):
        col = pl.multiple_of(lax.rem(s, nj) * tf, tf)
        slot = lax.rem(s, FFN_WEIGHT_SLOTS)
        return [pltpu.make_async_copy(wg_hbm.at[:, pl.ds(col, tf)], wg_buf.at[slot], sem.at[0, slot]),
                pltpu.make_async_copy(wu_hbm.at[:, pl.ds(col, tf)], wu_buf.at[slot], sem.at[1, slot]),
                pltpu.make_async_copy(wd_hbm.at[pl.ds(col, tf), :], wd_buf.at[slot], sem.at[2, slot])]

    @pl.when(step == 0)
    def _():
        for s0 in range(ahead):
            for cp in tile_copies(jnp.int32(s0)):
                cp.start()

    @pl.when(step + ahead < n_steps)
    def _():
        for cp in tile_copies(step + ahead):
            cp.start()

    @pl.when(j == 0)
    def _():
        xp = xp_ref[...]
        xs = xs_ref[...]
        g = g_ref[...]
        h_ref[:tp] = _rms_rows(xp, g).astype(BF16)
        h_ref[tp:] = _rms_rows(xs, g).astype(BF16)
        op_ref[...] = xp
        os_ref[...] = xs

    for cp in tile_copies(step):
        cp.wait()
    slot = lax.rem(step, FFN_WEIGHT_SLOTS)
    h = h_ref[...]
    a = jnp.dot(h, wg_buf[slot].astype(BF16), preferred_element_type=F32)
    u = jnp.dot(h, wu_buf[slot].astype(BF16), preferred_element_type=F32)
    t = (a * jax.nn.sigmoid(a) * (0.5 * u)).astype(BF16)
    y = jnp.dot(t, wd_buf[slot].astype(BF16), preferred_element_type=F32)
    op_ref[...] += y[:tp]
    os_ref[...] += y[tp:]


def _ffn(xp, xs, g, wg, wu, wd, tm, tf):
    m, d = xp.shape
    f = wg.shape[1]
    nt = m // tm
    ts = xs.shape[0] // nt
    assert nt * (f // tf) >= FFN_WEIGHT_SLOTS
    rows = lambda t: pl.BlockSpec((t, d), lambda i, j: (i, 0))
    acc = lambda t: pl.BlockSpec((t, d), lambda i, j: (i, 0), pipeline_mode=pl.Buffered(1))
    anyspec = pl.BlockSpec(memory_space=pl.ANY)
    return pl.pallas_call(
        functools.partial(_ffn_kernel, tf=tf),
        grid=(nt, f // tf),
        in_specs=[rows(tm), rows(ts), pl.BlockSpec((1, d), lambda i, j: (0, 0)), anyspec, anyspec, anyspec],
        out_specs=[acc(tm), acc(ts)],
        out_shape=[jax.ShapeDtypeStruct(xp.shape, F32), jax.ShapeDtypeStruct(xs.shape, F32)],
        scratch_shapes=[pltpu.VMEM((tm + ts, d), BF16),
                        pltpu.VMEM((FFN_WEIGHT_SLOTS, d, tf), F32),
                        pltpu.VMEM((FFN_WEIGHT_SLOTS, d, tf), F32),
                        pltpu.VMEM((FFN_WEIGHT_SLOTS, tf, d), F32),
                        pltpu.SemaphoreType.DMA((3, FFN_WEIGHT_SLOTS))],
        compiler_params=_cparams(("arbitrary", "arbitrary")),
        name="ffn",
    )(xp, xs, g, wg, wu, wd)


def _normmm_kernel(xp_ref, xs_ref, g_ref, w_ref, ap_ref, as_ref, gp_ref, gs_ref, h_ref, *, a_tiles):
    j = pl.program_id(1)
    tp = xp_ref.shape[0]

    @pl.when(j == 0)
    def _():
        g = g_ref[...]
        h_ref[:tp] = _rms_rows(xp_ref[...], g).astype(BF16)
        h_ref[tp:] = _rms_rows(xs_ref[...], g).astype(BF16)

    z = jnp.dot(h_ref[...], w_ref[...].astype(BF16), preferred_element_type=F32)

    @pl.when(j < a_tiles)
    def _():
        ap_ref[...] = z[:tp]
        as_ref[...] = z[tp:]

    @pl.when(j >= a_tiles)
    def _():
        gp_ref[...] = z[:tp].astype(BF16)
        gs_ref[...] = z[tp:].astype(BF16)


def _normmm(xp, xs, g, w, n_a, tm, tn):
    m, d = xp.shape
    n = w.shape[1]
    nt = m // tm
    ts = xs.shape[0] // nt
    a_tiles = n_a // tn
    rows = lambda t: pl.BlockSpec((t, d), lambda i, j: (i, 0), pipeline_mode=pl.Buffered(1))
    a_out = lambda t: pl.BlockSpec((t, tn), lambda i, j: (i, jnp.minimum(j, a_tiles - 1)))
    g_out = lambda t: pl.BlockSpec((t, tn), lambda i, j: (i, jnp.maximum(j - a_tiles, 0)))
    return pl.pallas_call(
        functools.partial(_normmm_kernel, a_tiles=a_tiles),
        grid=(nt, n // tn),
        in_specs=[
            rows(tm), rows(ts),
            pl.BlockSpec((1, d), lambda i, j: (0, 0)),
            pl.BlockSpec((d, tn), lambda i, j: (0, j)),
        ],
        out_specs=[a_out(tm), a_out(ts), g_out(tm), g_out(ts)],
        out_shape=[jax.ShapeDtypeStruct((m, n_a), F32), jax.ShapeDtypeStruct((xs.shape[0], n_a), F32),
                   jax.ShapeDtypeStruct((m, n - n_a), BF16), jax.ShapeDtypeStruct((xs.shape[0], n - n_a), BF16)],
        scratch_shapes=[pltpu.VMEM((tm + ts, d), BF16)],
        compiler_params=_cparams(("parallel", "arbitrary")),
        name="normmm",
    )(xp, xs, g, w)


def _group_ms(x):
    r = lax.broadcasted_iota(jnp.int32, (LANES, LANES), 0)
    c = lax.broadcasted_iota(jnp.int32, (LANES, LANES), 1)
    grp = jnp.where((r >> 6) == (c >> 6), 1.0, 0.0).astype(BF16)
    x2 = x * x
    hi = x2.astype(BF16)
    r1 = x2 - hi.astype(F32)
    mid = r1.astype(BF16)
    lo = (r1 - mid.astype(F32)).astype(BF16)
    s = (jnp.dot(hi, grp, preferred_element_type=F32)
         + jnp.dot(mid, grp, preferred_element_type=F32)
         + jnp.dot(lo, grp, preferred_element_type=F32))
    return s * (1.0 / HALF)


def _norm_rope(x, g, cos, sin):
    y = x * lax.rsqrt(_group_ms(x) + EPS) * g
    lane = lax.broadcasted_iota(jnp.int32, y.shape, 1)
    first = (lane & (HALF - 1)) < (HALF // 2)
    rot = jnp.where(first, -pltpu.roll(y, LANES - HALF // 2, 1), pltpu.roll(y, HALF // 2, 1))
    return y * cos + rot * sin


def _prep_prompt_kernel(q_ref, k_ref, v_ref, cos_ref, sin_ref, gq_ref, gk_ref,
                        qb_ref, kb_ref, vb_ref, kf_ref, vf_ref):
    cos = cos_ref[...]
    sin = sin_ref[...]
    gq = gq_ref[...]
    gk = gk_ref[...]
    vf_ref[...] = v_ref[...]
    for h in range(HEADS):
        sl = slice(h * HEAD_W, (h + 1) * HEAD_W)
        k = _norm_rope(k_ref[:, sl], gk, cos, sin)
        kf_ref[:, sl] = k
        kb_ref[h] = k.astype(BF16)
        qb_ref[h] = (_norm_rope(q_ref[:, sl], gq, cos, sin) * (HALF ** -0.5)).astype(BF16)
        vb_ref[h] = v_ref[:, sl].astype(BF16)


def _prep_prompt(z, cos, sin, gq, gk, batch, seq, tm):
    nt = seq // tm
    width = HEADS * HEAD_W
    row = lambda b, i: b * nt + i
    hm = pl.BlockSpec((None, HEADS, tm, HEAD_W), lambda b, i: (b, 0, i, 0))
    flat = pl.BlockSpec((tm, width), lambda b, i: (row(b, i), 0))
    vec = pl.BlockSpec((1, HEAD_W), lambda b, i: (0, 0))
    tab = pl.BlockSpec((tm, HEAD_W), lambda b, i: (i, 0))
    hm_shape = jax.ShapeDtypeStruct((batch, HEADS, seq, HEAD_W), BF16)
    flat_shape = jax.ShapeDtypeStruct((batch * seq, width), F32)
    return pl.pallas_call(
        _prep_prompt_kernel,
        grid=(batch, nt),
        in_specs=[
            pl.BlockSpec((tm, width), lambda b, i: (row(b, i), 0)),
            pl.BlockSpec((tm, width), lambda b, i: (row(b, i), 1)),
            pl.BlockSpec((tm, width), lambda b, i: (row(b, i), 2)),
            tab, tab, vec, vec,
        ],
        out_specs=[hm, hm, hm, flat, flat],
        out_shape=[hm_shape, hm_shape, hm_shape, flat_shape, flat_shape],
        compiler_params=_cparams(("parallel", "parallel")),
        name="prep_prompt",
    )(z, z, z, cos, sin, gq, gk)


def _prep_sample_kernel(q_ref, k_ref, v_ref, cos_ref, sin_ref, gq_ref, gk_ref,
                        qf_ref, kf_ref, vf_ref):
    cos = cos_ref[...]
    sin = sin_ref[...]
    qf_ref[...] = _norm_rope(q_ref[...], gq_ref[...], cos, sin) * (HALF ** -0.5)
    kf_ref[...] = _norm_rope(k_ref[...], gk_ref[...], cos, sin)
    vf_ref[...] = v_ref[...]


def _prep_sample(z, cos, sin, gq, gk):
    m = z.shape[0]
    blk = lambda off: pl.BlockSpec((m, HEAD_W), lambda h: (0, off + h))
    vec = pl.BlockSpec((1, HEAD_W), lambda h: (0, 0))
    tab = pl.BlockSpec((m, HEAD_W), lambda h: (0, 0))
    shape = jax.ShapeDtypeStruct((m, HEADS * HEAD_W), F32)
    return pl.pallas_call(
        _prep_sample_kernel,
        grid=(HEADS,),
        in_specs=[blk(0), blk(HEADS), blk(2 * HEADS), tab, tab, vec, vec],
        out_specs=[blk(0), blk(0), blk(0)],
        out_shape=[shape, shape, shape],
        compiler_params=_cparams(("parallel",)),
        name="prep_sample",
    )(z, z, z, cos, sin, gq, gk)


def _lambda(lam_ref):
    l = lam_ref[...]
    s1 = jnp.sum(l[0:1] * l[1:2], axis=1, keepdims=True)
    s2 = jnp.sum(l[2:3] * l[3:4], axis=1, keepdims=True)
    return jnp.exp(s1) - jnp.exp(s2) + LAM_INIT


def _prompt_block(qi, lam, q_ref, k_ref, v_ref, o_ref, m_ref, l_ref, acc_ref, tq):
    q = q_ref[...]
    lane = lax.broadcasted_iota(jnp.int32, q.shape, 1)
    zero = jnp.zeros_like(q)
    qs = jnp.concatenate([jnp.where(lane < HALF, q, zero),
                          jnp.where(lane >= HALF, q, zero)], axis=0)

    m_ref[...] = jnp.full(m_ref.shape, NEG, F32)
    l_ref[...] = jnp.zeros(l_ref.shape, F32)
    acc_ref[...] = jnp.zeros(acc_ref.shape, F32)

    def step(kb, masked):
        start = pl.multiple_of(kb * tq, tq)
        k = k_ref[pl.ds(start, tq), :]
        v = v_ref[pl.ds(start, tq), :]
        s = _nt_dot(qs, k)
        if masked:
            r = lax.broadcasted_iota(jnp.int32, s.shape, 0)
            c = lax.broadcasted_iota(jnp.int32, s.shape, 1)
            r = jnp.where(r >= tq, r - tq, r)
            s = jnp.where(r >= c, s, NEG)
        m_prev = m_ref[...]
        m_new = jnp.maximum(m_prev, jnp.max(s, axis=1, keepdims=True))
        alpha = jnp.exp(m_prev - m_new)
        p = jnp.exp(s - jnp.concatenate([m_new] * (tq // LANES), axis=1))
        lsum = p[:, 0:LANES]
        for cblk in range(1, tq // LANES):
            lsum = lsum + p[:, cblk * LANES:(cblk + 1) * LANES]
        l_ref[...] = alpha * l_ref[...] + lsum
        acc_ref[...] = alpha * acc_ref[...] + jnp.dot(p.astype(BF16), v, preferred_element_type=F32)
        m_ref[...] = m_new

    def body(pair, carry):
        step(2 * pair, False)
        step(2 * pair + 1, False)
        return carry

    lax.fori_loop(0, qi // 2, body, 0)

    @pl.when(qi % 2 == 1)
    def _():
        step(qi - 1, False)

    step(qi, True)

    l = jnp.sum(l_ref[...], axis=1, keepdims=True)
    o = acc_ref[...] / l
    o_ref[...] = o[:tq] - lam * o[tq:]


def _sample_seq(lam, q, kn, vn, kpast, vpast):
    q2 = jnp.concatenate([q, q], axis=0)
    row = lax.broadcasted_iota(jnp.int32, q2.shape, 0)
    lane = lax.broadcasted_iota(jnp.int32, q2.shape, 1)
    qm = jnp.where((lane >> 6) == (row >> 3), q2, 0.0)

    s = _nt_dot(qm.astype(BF16), kpast.astype(BF16))
    own = ((lax.broadcasted_iota(jnp.int32, s.shape, 1) & (HEADS - 1))
           == (lax.broadcasted_iota(jnp.int32, s.shape, 0) & (HEADS - 1)))
    s = jnp.where(own, s, NEG)
    s_new = jnp.sum(qm * jnp.concatenate([kn, kn], axis=0), axis=1, keepdims=True)
    m = jnp.maximum(jnp.max(s, axis=1, keepdims=True), s_new)
    p = jnp.exp(s - m)
    p_new = jnp.exp(s_new - m)
    inv = 1.0 / (jnp.sum(p, axis=1, keepdims=True) + p_new)
    c0 = inv[:HEADS]
    c1 = lam * inv[HEADS:]
    pd = p[:HEADS] * c0 - p[HEADS:] * c1
    pd_new = p_new[:HEADS] * c0 - p_new[HEADS:] * c1
    return jnp.dot(pd.astype(BF16), vpast.astype(BF16), preferred_element_type=F32) + pd_new * vn


def _causal_order(step, n):
    return jnp.where(step % 2 == 0, n - 1 - step // 2, step // 2)


def _attn_kernel(pt_ref, lam_ref, q_ref, k_ref, v_ref, qs_ref, kn_ref, vn_ref, ck_hbm, cv_hbm,
                 o_ref, os_ref, m_ref, l_ref, acc_ref, kbuf, vbuf, sem, *, tq, n_pages):
    step = pl.program_id(2)
    qi = _causal_order(step, pl.num_programs(2))
    seq = (pl.program_id(0) * pl.num_programs(1) + pl.program_id(1)) * pl.num_programs(2) + step
    n_seq = pl.num_programs(0) * pl.num_programs(1) * pl.num_programs(2)
    slot = seq % 2
    page_rows = PAGE * HEADS

    def page_copies(seq_, slot_):
        cps = []
        for p in range(n_pages):
            pg = pt_ref[seq_, p]
            dst = pl.ds(p * page_rows, page_rows)
            cps.append(pltpu.make_async_copy(ck_hbm.at[pg], kbuf.at[slot_, dst, :], sem.at[0, slot_]))
            cps.append(pltpu.make_async_copy(cv_hbm.at[pg], vbuf.at[slot_, dst, :], sem.at[1, slot_]))
        return cps

    @pl.when(seq == 0)
    def _():
        for cp in page_copies(0, 0):
            cp.start()

    @pl.when(seq + 1 < n_seq)
    def _():
        for cp in page_copies(seq + 1, 1 - slot):
            cp.start()

    lam = _lambda(lam_ref)
    _prompt_block(qi, lam, q_ref, k_ref, v_ref, o_ref, m_ref, l_ref, acc_ref, tq)

    for cp in page_copies(seq, slot):
        cp.wait()
    os_ref[...] = _sample_seq(lam, qs_ref[...], kn_ref[...], vn_ref[...], kbuf[slot], vbuf[slot])


def _attn(page_table, lam, qb, kb, vb, qs, kn, vn, ck, cv, tq):
    batch, heads, seq, w = qb.shape
    nq = seq // tq
    nb, n_pages = page_table.shape
    assert batch * heads * nq == nb, "one sample sequence per prompt query block"
    rows = n_pages * PAGE * HEADS
    sidx = lambda b, h, i, pt: ((b * heads + h) * nq + i, 0, 0)
    tile = pl.BlockSpec((None, HEADS, HEAD_W), sidx)
    kv = pl.BlockSpec((None, None, seq, w), lambda b, h, i, pt: (b, h, 0, 0))
    grid_spec = pltpu.PrefetchScalarGridSpec(
        num_scalar_prefetch=1,
        grid=(batch, heads, nq),
        in_specs=[
            pl.BlockSpec(lam.shape, lambda b, h, i, pt: (0, 0)),
            pl.BlockSpec((None, None, tq, w), lambda b, h, i, pt: (b, h, _causal_order(i, nq), 0)),
            kv, kv,
            tile, tile, tile,
            pl.BlockSpec(memory_space=pl.ANY),
            pl.BlockSpec(memory_space=pl.ANY),
        ],
        out_specs=[pl.BlockSpec((tq, w), lambda b, h, i, pt: (b * nq + _causal_order(i, nq), h)), tile],
        scratch_shapes=[pltpu.VMEM((2 * tq, LANES), F32),
                        pltpu.VMEM((2 * tq, LANES), F32),
                        pltpu.VMEM((2 * tq, w), F32),
                        pltpu.VMEM((2, rows, HEAD_W), F32),
                        pltpu.VMEM((2, rows, HEAD_W), F32),
                        pltpu.SemaphoreType.DMA((2, 2))],
    )
    return pl.pallas_call(
        functools.partial(_attn_kernel, tq=tq, n_pages=n_pages),
        grid_spec=grid_spec,
        out_shape=[jax.ShapeDtypeStruct((batch * seq, heads * w), F32),
                   jax.ShapeDtypeStruct((nb, HEADS, HEAD_W), F32)],
        compiler_params=_cparams(("arbitrary", "arbitrary", "arbitrary")),
        name="attn",
    )(page_table, lam, qb, kb, vb, qs, kn, vn, ck, cv)


def _lower_bound(lb_ref, sl):
    a = lb_ref[:, sl]
    mx = jnp.maximum(a[0:1], a[1:2])
    e0 = jnp.exp(a[0:1] - mx)
    e1 = jnp.exp(a[1:2] - mx)
    return e0 / (e0 + e1)


def _split3(x):
    hi = x.astype(BF16)
    r1 = x - hi.astype(F32)
    mid = r1.astype(BF16)
    lo = (r1 - mid.astype(F32)).astype(BF16)
    return hi, mid, lo


def _pair_ref(b, m, c):
    parts = [jnp.broadcast_to(b[2 * m * p + m - 1:2 * m * p + m, :], (2 * m, b.shape[1]))
             for p in range(c // (2 * m))]
    return parts[0] if len(parts) == 1 else jnp.concatenate(parts, axis=0)


def _hgrn_prompt_kernel(lb_ref, q_ref, f_ref, i_ref, o_ref, s_out_ref, st_ref, *, c):
    ci = pl.program_id(1)

    @pl.when(ci == 0)
    def _():
        st_ref[...] = jnp.zeros(st_ref.shape, F32)

    row = lax.broadcasted_iota(jnp.int32, (c, c), 0)
    col = lax.broadcasted_iota(jnp.int32, (c, c), 1)
    tri = jnp.where(col <= row, 1.0, 0.0).astype(BF16)
    same_sub = (row >> 4) == (col >> 4)
    causal = col <= row
    level_masks = {16: (row >> 5) == (col >> 5), 32: (row >> 6) == (col >> 6), 64: None}
    width = HEADS * HEAD_W
    trow = lax.broadcasted_iota(jnp.int32, (c, width), 0)
    sub = 16

    lb = _lower_bound(lb_ref, slice(None))
    q = q_ref[...]
    v = i_ref[...]
    f = lb + (1.0 - lb) * jax.nn.sigmoid(f_ref[...])
    k = 1.0 - f
    hi, mid, lo = _split3(jnp.log(f))
    b = (jnp.dot(tri, hi, preferred_element_type=F32)
         + jnp.dot(tri, mid, preferred_element_type=F32)
         + jnp.dot(tri, lo, preferred_element_type=F32))

    ref_d = jnp.concatenate(
        [jnp.zeros((sub, width), F32)]
        + [jnp.broadcast_to(b[sub * j - 1:sub * j, :], (sub, width)) for j in range(1, c // sub)],
        axis=0)
    arg_d = b - ref_d
    q_d = (q * jnp.exp(arg_d)).astype(BF16)
    k_d = (k * jnp.exp(-arg_d)).astype(BF16)
    levels = []
    for m in (16, 32, 64):
        odd = ((trow >> int(math.log2(m))) & 1) == 1
        d = b - _pair_ref(b, m, c)
        e = jnp.exp(jnp.where(odd, d, -d))
        levels.append((level_masks[m],
                       jnp.where(odd, q * e, 0.0).astype(BF16),
                       jnp.where(odd, 0.0, k * e).astype(BF16)))
    b_last = b[c - 1:c, :]
    q_t = (q * jnp.exp(b)).astype(BF16)
    k_t = (k * jnp.exp(b_last - b)).astype(BF16)
    decay = jnp.exp(b_last)
    v_b = v.astype(BF16)

    for h in range(HEADS):
        sl = slice(h * HEAD_W, (h + 1) * HEAD_W)
        att = jnp.where(same_sub, jnp.where(causal, _nt_dot(q_d[:, sl], k_d[:, sl]), 0.0), 0.0)
        for msk, q_m, k_m in levels:
            p_m = _nt_dot(q_m[:, sl], k_m[:, sl])
            att = att + (p_m if msk is None else jnp.where(msk, p_m, 0.0))
        st = st_ref[h]
        o_ref[:, sl] = (jnp.dot(att.astype(BF16), v_b[:, sl], preferred_element_type=F32)
                        + _nt_dot(q_t[:, sl], st.astype(BF16)))
        st_ref[h] = st * decay[:, sl] + jnp.dot(v[:, sl].T.astype(BF16), k_t[:, sl],
                                                preferred_element_type=F32)

    @pl.when(ci == pl.num_programs(1) - 1)
    def _():
        for h in range(HEADS):
            s_out_ref[h] = st_ref[h].T


def _hgrn_prompt(lb2, z, batch, seq, c):
    nc = seq // c
    width = HEADS * HEAD_W
    col = lambda off: pl.BlockSpec((c, width), lambda b, i: (b * nc + i, off))
    return pl.pallas_call(
        functools.partial(_hgrn_prompt_kernel, c=c),
        grid=(batch, nc),
        in_specs=[pl.BlockSpec(lb2.shape, lambda b, i: (0, 0)), col(3), col(4), col(5)],
        out_specs=[pl.BlockSpec((c, width), lambda b, i: (b * nc + i, 0)),
                   pl.BlockSpec((None, HEADS, HEAD_W, HEAD_W), lambda b, i: (b, 0, 0, 0))],
        out_shape=[jax.ShapeDtypeStruct((batch * seq, width), F32),
                   jax.ShapeDtypeStruct((batch, HEADS, HEAD_W, HEAD_W), F32)],
        scratch_shapes=[pltpu.VMEM((HEADS, HEAD_W, HEAD_W), F32)],
        compiler_params=_cparams(("parallel", "arbitrary")),
        name="hgrn_prompt",
    )(lb2, z, z, z)


def _hgrn_sample_kernel(lb_ref, q_ref, f_ref, i_ref, s_ref, o_ref, s_out_ref, *, rows):
    r_i = lax.broadcasted_iota(jnp.int32, (HEAD_W, HEAD_W), 0)
    c_i = lax.broadcasted_iota(jnp.int32, (HEAD_W, HEAD_W), 1)
    eye = r_i == c_i

    def to_col(x):
        return jnp.sum(jnp.where(eye, x, 0.0), axis=1, keepdims=True)

    for h in range(HEADS):
        sl = slice(h * HEAD_W, (h + 1) * HEAD_W)
        lb = _lower_bound(lb_ref, sl)
        f_all = lb + (1.0 - lb) * jax.nn.sigmoid(f_ref[:, sl])
        q_all = q_ref[:, sl]
        i_all = i_ref[:, sl]
        o_rows = []
        for r in range(rows):
            f_col = to_col(f_all[r:r + 1])
            q_col = to_col(q_all[r:r + 1])
            s_new = f_col * s_ref[r, h] + (1.0 - f_col) * i_all[r:r + 1]
            s_out_ref[r, h] = s_new
            o_rows.append(jnp.sum(q_col * s_new, axis=0, keepdims=True))
        o_ref[:, sl] = jnp.concatenate(o_rows, axis=0)


def _hgrn_sample(lb2, z, state, rows):
    nb = state.shape[0]
    width = HEADS * HEAD_W
    col = lambda off: pl.BlockSpec((rows, width), lambda i: (i, off))
    sspec = pl.BlockSpec((rows, HEADS, HEAD_W, HEAD_W), lambda i: (i, 0, 0, 0))
    return pl.pallas_call(
        functools.partial(_hgrn_sample_kernel, rows=rows),
        grid=(nb // rows,),
        in_specs=[pl.BlockSpec(lb2.shape, lambda i: (0, 0)), col(3), col(4), col(5), sspec],
        out_specs=[pl.BlockSpec((rows, width), lambda i: (i, 0)), sspec],
        out_shape=[jax.ShapeDtypeStruct((nb, width), F32),
                   jax.ShapeDtypeStruct(state.shape, F32)],
        compiler_params=_cparams(("parallel",)),
        name="hgrn_sample",
    )(lb2, z, z, z, state)


W_CHUNK = 512


def _head_rms(x, g):
    parts = []
    for h in range(HEADS):
        blk = x[:, h * HEAD_W:(h + 1) * HEAD_W]
        parts.append(_rms_rows(blk, g))
    return jnp.concatenate(parts, axis=1)


def _mixout_kernel(oa_ref, ob_ref, zg_ref, x_ref, oas_ref, obs_ref, zgs_ref, xs_ref, na_ref, nb_ref,
                   wa_hbm, wb_hbm, wo_hbm, o_ref, os_ref, wa_v, wb_v, wo_v, stage, sem):
    @pl.when(pl.program_id(0) == 0)
    def _():
        chunks = [(src, dst, r) for src, dst in ((wa_hbm, wa_v), (wb_hbm, wb_v), (wo_hbm, wo_v))
                  for r in range(0, src.shape[0], W_CHUNK)]

        def copy(c):
            src, _, r = chunks[c]
            return pltpu.make_async_copy(src.at[pl.ds(r, W_CHUNK), :], stage.at[c % 2], sem.at[c % 2])

        copy(0).start()
        for c in range(len(chunks)):
            if c + 1 < len(chunks):
                copy(c + 1).start()
            copy(c).wait()
            _, dst, r = chunks[c]
            dst[pl.ds(r, W_CHUNK), :] = stage[c % 2].astype(BF16)

    def mix(oa_r, ob_r, zg_r, x_r, out_r):
        w = oa_r.shape[1]
        n = out_r.shape[1]
        a = (_head_rms(oa_r[...], na_ref[...]) * (1.0 - LAM_INIT)).astype(BF16)
        og = zg_r[:, :w].astype(F32)
        b = (_head_rms(ob_r[...], nb_ref[...]) * (og * jax.nn.sigmoid(og))).astype(BF16)
        pa = jnp.dot(a, wa_v[...], preferred_element_type=F32)
        pb = jnp.dot(b, wb_v[...], preferred_element_type=F32)
        mg = (jax.nn.sigmoid(zg_r[:, w:w + n].astype(F32)) * pa
              + jax.nn.sigmoid(zg_r[:, w + n:].astype(F32)) * pb).astype(BF16)
        out_r[...] = x_r[...] + jnp.dot(mg, wo_v[...], preferred_element_type=F32)

    mix(oa_ref, ob_ref, zg_ref, x_ref, o_ref)

    @pl.when(pl.program_id(0) == pl.num_programs(0) - 1)
    def _():
        mix(oas_ref, obs_ref, zgs_ref, xs_ref, os_ref)


def _mixout(oa, ob, zg, x, oas, obs, zgs, xs, na, nb, wa, wb, wo, tm):
    m, w = oa.shape
    n = wo.shape[1]
    assert w % W_CHUNK == 0 and n % W_CHUNK == 0 and wa.shape[1] == n
    rows = lambda c: pl.BlockSpec((tm, c), lambda i: (i, 0))
    whole = lambda a: pl.BlockSpec(a.shape, lambda i: (0, 0))
    vec = pl.BlockSpec((1, HEAD_W), lambda i: (0, 0))
    anyspec = pl.BlockSpec(memory_space=pl.ANY)
    return pl.pallas_call(
        _mixout_kernel,
        grid=(m // tm,),
        in_specs=[rows(w), rows(w), rows(zg.shape[1]), rows(n),
                  whole(oas), whole(obs), whole(zgs), whole(xs),
                  vec, vec, anyspec, anyspec, anyspec],
        out_specs=[rows(n), whole(xs)],
        out_shape=[jax.ShapeDtypeStruct((m, n), F32), jax.ShapeDtypeStruct(xs.shape, F32)],
        scratch_shapes=[pltpu.VMEM((w, n), BF16), pltpu.VMEM((w, n), BF16), pltpu.VMEM((n, n), BF16),
                        pltpu.VMEM((2, W_CHUNK, n), F32), pltpu.SemaphoreType.DMA((2,))],
        compiler_params=_cparams(("arbitrary",)),
        name="mixout",
    )(oa, ob, zg, x, oas, obs, zgs, xs, na, nb, wa, wb, wo)


def _rope_tables(pos):
    half = HALF // 2
    inv = ROPE_THETA ** (-jnp.arange(half, dtype=F32) / half)
    ang = pos.astype(F32)[:, None] * inv[None, :]
    cos = jnp.tile(jnp.cos(ang), (1, LANES // half))
    sin = jnp.tile(jnp.sin(ang), (1, LANES // half))
    return cos, sin


def kernel(x_prompt, x_sample, cache_k, cache_v, state_hgrn, page_table, ffn1_norm, ffn1_w_gate, ffn1_w_up, ffn1_w_down, mix_norm, w_in, q_norm, k_norm, lambda_q1, lambda_k1, lambda_q2, lambda_k2, attn_sub_norm, hgrn_lower_bounds, hgrn_out_norm, w_proj_a, w_proj_b, w_out, ffn2_norm, ffn2_w_gate, ffn2_w_up, ffn2_w_down):
    batch, seq, d = x_prompt.shape
    nb = x_sample.shape[0]
    n_pages = page_table.shape[1]
    width = HEADS * HEAD_W
    n_a = 6 * width

    w1g, w1u, w1d = ffn1_w_gate[0], ffn1_w_up[0], ffn1_w_down[0]
    w2g, w2u, w2d = ffn2_w_gate[0], ffn2_w_up[0], ffn2_w_down[0]
    w_in_b = w_in[0]
    wa, wb, wo = w_proj_a[0], w_proj_b[0], w_out[0]
    lam = jnp.concatenate([lambda_q1, lambda_k1, lambda_q2, lambda_k2], axis=0).astype(F32)
    gq = jnp.tile(q_norm.astype(F32), (1, 2))
    gk = jnp.tile(k_norm.astype(F32), (1, 2))
    lb2 = hgrn_lower_bounds.astype(F32)

    xp = x_prompt.reshape(batch * seq, d)
    xs = x_sample.reshape(nb, d)

    x1p, x1s = _ffn(xp, xs, ffn1_norm, w1g, w1u, w1d, FFN_ROWS, FFN_HIDDEN)
    zap, zas, zgp, zgs = _normmm(x1p, x1s, mix_norm, w_in_b, n_a, INPROJ_ROWS, INPROJ_COLS)
    cos_p, sin_p = _rope_tables(jnp.arange(seq))
    qb, kb, vb, kf, vf = _prep_prompt(zap, cos_p, sin_p, gq, gk, batch, seq, PREP_ROWS)
    pos_s = jnp.full((nb,), n_pages * PAGE, jnp.int32)
    cos_s, sin_s = _rope_tables(pos_s)
    qs, ks, vs = _prep_sample(zas, cos_s, sin_s, gq, gk)

    ck = cache_k.reshape(cache_k.shape[1], PAGE * HEADS, HEAD_W)
    cv = cache_v.reshape(cache_v.shape[1], PAGE * HEADS, HEAD_W)
    tiles = lambda a: a.reshape(nb, HEADS, HEAD_W)
    oa_p, oa_s = _attn(page_table, lam, qb, kb, vb, tiles(qs), tiles(ks), tiles(vs), ck, cv, ATTN_BLOCK)
    oa_s = oa_s.reshape(nb, width)
    ob_p, st_p = _hgrn_prompt(lb2, zap, batch, seq, HGRN_CHUNK)
    ob_s, st_s = _hgrn_sample(lb2, zas, state_hgrn[0], HGRN_SAMPLE_ROWS)

    x2p, x2s = _mixout(oa_p, ob_p, zgp, x1p, oa_s, ob_s, zgs, x1s,
                       attn_sub_norm, hgrn_out_norm, wa, wb, wo, MIX_ROWS)
    y_p, y_s = _ffn(x2p, x2s, ffn2_norm, w2g, w2u, w2d, FFN_ROWS, FFN_HIDDEN)

    return (y_p.reshape(batch, seq, d),
            y_s.reshape(nb, 1, d),
            kf.reshape(1, batch, seq, HEADS, HEAD_W),
            vf.reshape(1, batch, seq, HEADS, HEAD_W),
            ks.reshape(1, nb, 1, HEADS, HEAD_W),
            vs.reshape(1, nb, 1, HEADS, HEAD_W),
            st_p.reshape(1, batch, HEADS, HEAD_W, HEAD_W),
            st_s.reshape(1, nb, HEADS, HEAD_W, HEAD_W))
```
